```python
import jax, jax.numpy as jnp
from jax import lax
import numpy as np

D_MODEL = 4096
BATCH = 4
SEQ = 2048
DEPTH = 1
DEC_BATCH = 128
DEC_SEQ = 4
PAST_LEN = 16384
PAGE_SIZE = 128

MIX_WIDTH = D_MODEL
CONV_WIDTH = MIX_WIDTH // 2
CONV_KERNEL = 31
MLSTM_WIDTH = MIX_WIDTH - CONV_WIDTH
MLSTM_HEADS = 4
MLSTM_DV = MLSTM_WIDTH // MLSTM_HEADS
MLSTM_DK = MLSTM_DV // 2
MLSTM_CHUNK = 64
IN_COLS = 2 * CONV_WIDTH + 2 * MLSTM_HEADS * MLSTM_DK + 2 * MLSTM_WIDTH + 2 * MLSTM_HEADS
N_EXPERTS = 32
TOP_K = 4
D_FF = D_MODEL
SWIGLU_LIMIT = 7.0
SWIGLU_ALPHA = 1.702
NORM_EPS = 1e-5

kernel_name = "hymba_conformer_mlstm_moe_step"


def rmsnorm(x, g):
    xf = x.astype(jnp.float32)
    y = xf * lax.rsqrt(jnp.mean(xf * xf, axis=-1, keepdims=True) + NORM_EPS)
    return (y * g.astype(jnp.float32)).astype(x.dtype)


def layernorm(x, g, b):
    xf = x.astype(jnp.float32)
    mu = jnp.mean(xf, axis=-1, keepdims=True)
    xc = xf - mu
    var = jnp.mean(xc * xc, axis=-1, keepdims=True)
    y = xc * lax.rsqrt(var + NORM_EPS) * g.astype(jnp.float32) + b.astype(jnp.float32)
    return y.astype(x.dtype)


def conformer_conv(u_a, u_b, buf, w_dw, b_dw, ln_g, ln_b):
    glu = u_a * jax.nn.sigmoid(u_b)
    full = jnp.concatenate([buf.astype(glu.dtype), glu], axis=1)
    c = full.shape[-1]
    y = lax.conv_general_dilated(
        full, w_dw[:, None, :].astype(full.dtype), window_strides=(1,), padding='VALID',
        dimension_numbers=('NWC', 'WIO', 'NWC'), feature_group_count=c) + b_dw.astype(full.dtype)
    y = jax.nn.silu(layernorm(y, ln_g, ln_b))
    new_buf = full[:, -(CONV_KERNEL - 1):, :]
    return y, new_buf


def mlstm_chunkwise(q, k, v, i_pre, f_pre, C0, n0, m0):
    f32 = jnp.float32
    B, L, H, DK = q.shape
    DV = v.shape[-1]
    Lc = MLSTM_CHUNK if L % MLSTM_CHUNK == 0 else L
    nc = L // Lc
    q = q.astype(f32) * (DK ** -0.5)

    def chunks(a):
        a = a.astype(f32)
        return jnp.moveaxis(a.reshape((B, nc, Lc) + a.shape[2:]), 1, 0)

    qc, kc, vc, ic, fc = chunks(q), chunks(k), chunks(v), chunks(i_pre), chunks(f_pre)
    causal = jnp.tril(jnp.ones((Lc, Lc), dtype=bool))

    def step(carry, inp):
        C, n, m = carry
        qj, kj, vj, ij, fj = inp
        a = jnp.moveaxis(jnp.cumsum(jax.nn.log_sigmoid(fj), axis=1), 1, 2)
        ih = jnp.moveaxis(ij, 1, 2)
        d_log = a[..., :, None] - a[..., None, :] + ih[..., None, :]
        d_log = jnp.where(causal, d_log, -jnp.inf)
        inter = a + m[..., None]
        m_row = jnp.maximum(inter, jnp.max(d_log, axis=-1))
        w_intra = jnp.exp(d_log - m_row[..., None])
        w_inter = jnp.exp(inter - m_row)
        s = jnp.einsum('blhk,bshk->bhls', qj, kj) * w_intra
        num = (jnp.einsum('bhls,bshv->bhlv', s, vj)
               + w_inter[..., None] * jnp.einsum('blhk,bhkv->bhlv', qj, C))
        den_raw = jnp.sum(s, axis=-1) + w_inter * jnp.einsum('blhk,bhk->bhl', qj, n)
        den = jnp.maximum(jnp.abs(den_raw), jnp.exp(-m_row))
        h = num / den[..., None]
        a_end = a[..., -1]
        m_new = m_row[..., -1]
        decay = jnp.exp(a_end + m - m_new)
        w_end = jnp.exp(a_end[..., None] - a + ih - m_new[..., None])
        C_new = decay[..., None, None] * C + jnp.einsum('bhs,bshk,bshv->bhkv', w_end, kj, vj)
        n_new = decay[..., None] * n + jnp.einsum('bhs,bshk->bhk', w_end, kj)
        return (C_new, n_new, m_new), h

    (C_f, n_f, m_f), hs = lax.scan(step, (C0.astype(f32), n0.astype(f32), m0.astype(f32)),
                                   (qc, kc, vc, ic, fc))
    h = jnp.transpose(hs, (1, 0, 3, 2, 4)).reshape(B, L, H, DV)
    return h, C_f, n_f, m_f


def mixer(xn, conv_buf, C0, n0, m0, w_in, b_i, b_f, w_dw, b_dw, conv_ln_g, conv_ln_b,
          mlstm_norm_g, w_out):
    B, L, _ = xn.shape
    H, DK, DV = MLSTM_HEADS, MLSTM_DK, MLSTM_DV
    u = xn @ w_in
    o0 = 0
    u_a = u[..., o0:o0 + CONV_WIDTH]; o0 += CONV_WIDTH
    u_b = u[..., o0:o0 + CONV_WIDTH]; o0 += CONV_WIDTH
    q = u[..., o0:o0 + H * DK].reshape(B, L, H, DK); o0 += H * DK
    k = u[..., o0:o0 + H * DK].reshape(B, L, H, DK); o0 += H * DK
    v = u[..., o0:o0 + MLSTM_WIDTH].reshape(B, L, H, DV); o0 += MLSTM_WIDTH
    o_pre = u[..., o0:o0 + MLSTM_WIDTH]; o0 += MLSTM_WIDTH
    i_pre = u[..., o0:o0 + H].astype(jnp.float32) + b_i.astype(jnp.float32); o0 += H
    f_pre = u[..., o0:o0 + H].astype(jnp.float32) + b_f.astype(jnp.float32)
    conv_out, new_buf = conformer_conv(u_a, u_b, conv_buf, w_dw, b_dw, conv_ln_g, conv_ln_b)
    h, C_f, n_f, m_f = mlstm_chunkwise(q, k, v, i_pre, f_pre, C0, n0, m0)
    h = h * lax.rsqrt(jnp.mean(h * h, axis=-1, keepdims=True) + NORM_EPS)
    h = h * mlstm_norm_g.astype(jnp.float32).reshape(H, DV)
    h = h.reshape(B, L, MLSTM_WIDTH).astype(xn.dtype) * jax.nn.sigmoid(o_pre)
    mix = jnp.concatenate([conv_out, h], axis=-1)
    return mix @ w_out, new_buf, C_f, n_f, m_f


def moe(x, w_router, b_router, w_gate_up, b_gate_up, w_down, b_down):
    logits = (x @ w_router + b_router).astype(jnp.float32)
    top_val, top_idx = lax.top_k(logits, TOP_K)
    gates = jax.nn.softmax(top_val, axis=-1)
    combine = jnp.sum(jax.nn.one_hot(top_idx, N_EXPERTS, dtype=jnp.float32) * gates[..., None], axis=1)
    combine = combine.astype(x.dtype)
    y = jnp.zeros_like(x)
    for e in range(N_EXPERTS):
        gu = x @ w_gate_up[e] + b_gate_up[e]
        g = jnp.minimum(gu[:, :D_FF], SWIGLU_LIMIT)
        up = jnp.clip(gu[:, D_FF:], -SWIGLU_LIMIT, SWIGLU_LIMIT)
        act = (up + 1.0) * (g * jax.nn.sigmoid(SWIGLU_ALPHA * g))
        y = y + combine[:, e:e + 1] * (act @ w_down[e] + b_down[e])
    return y


def setup_inputs(seed: int = 0) -> dict:
    key = jax.random.key(seed)
    ks = jax.random.split(key, 24)
    f32 = jnp.float32
    H, DK, DV = MLSTM_HEADS, MLSTM_DK, MLSTM_DV
    nrm = lambda k, shape, s: jax.random.normal(k, shape, f32) * s
    return {
        "x_prompt": nrm(ks[0], (BATCH, SEQ, D_MODEL), 1.0),
        "x_sample": nrm(ks[1], (DEC_BATCH, DEC_SEQ, D_MODEL), 1.0),
        "state_conv": nrm(ks[2], (DEPTH, DEC_BATCH, CONV_KERNEL - 1, CONV_WIDTH), 0.5),
        "state_C": nrm(ks[3], (DEPTH, DEC_BATCH, H, DK, DV), 0.05),
        "state_n": nrm(ks[4], (DEPTH, DEC_BATCH, H, DK), 0.5),
        "state_m": nrm(ks[5], (DEPTH, DEC_BATCH, H), 1.0),
        "g_mix": 1.0 + nrm(ks[6], (DEPTH, D_MODEL), 0.02),
        "w_in": nrm(ks[7], (DEPTH, D_MODEL, IN_COLS), D_MODEL ** -0.5),
        "b_i": nrm(ks[8], (DEPTH, H), 0.1),
        "b_f": 3.0 + nrm(ks[9], (DEPTH, H), 0.5),
        "w_dw": nrm(ks[10], (DEPTH, CONV_KERNEL, CONV_WIDTH), CONV_KERNEL ** -0.5),
        "b_dw": nrm(ks[11], (DEPTH, CONV_WIDTH), 0.02),
        "conv_ln_g": 1.0 + nrm(ks[12], (DEPTH, CONV_WIDTH), 0.02),
        "conv_ln_b": nrm(ks[13], (DEPTH, CONV_WIDTH), 0.02),
        "mlstm_norm_g": 1.0 + nrm(ks[14], (DEPTH, MLSTM_WIDTH), 0.02),
        "w_out": nrm(ks[15], (DEPTH, MIX_WIDTH, D_MODEL), MIX_WIDTH ** -0.5),
        "g_ffn": 1.0 + nrm(ks[16], (DEPTH, D_MODEL), 0.02),
        "w_router": nrm(ks[17], (DEPTH, D_MODEL, N_EXPERTS), D_MODEL ** -0.5),
        "b_router": nrm(ks[18], (DEPTH, N_EXPERTS), 0.01),
        "w_gate_up": nrm(ks[19], (DEPTH, N_EXPERTS, D_MODEL, 2 * D_FF), D_MODEL ** -0.5),
        "b_gate_up": nrm(ks[20], (DEPTH, N_EXPERTS, 2 * D_FF), 0.02),
        "w_down": nrm(ks[21], (DEPTH, N_EXPERTS, D_FF, D_MODEL), D_FF ** -0.5),
        "b_down": nrm(ks[22], (DEPTH, N_EXPERTS, D_MODEL), 0.02),
        "g_final": 1.0 + nrm(ks[23], (D_MODEL,), 0.02),
    }


def reference(x_prompt, x_sample, state_conv, state_C, state_n, state_m, g_mix, w_in, b_i, b_f,
              w_dw, b_dw, conv_ln_g, conv_ln_b, mlstm_norm_g, w_out, g_ffn, w_router, b_router,
              w_gate_up, b_gate_up, w_down, b_down, g_final):
    B, S, D = x_prompt.shape
    Bd, Sd, _ = x_sample.shape
    H, DK, DV = MLSTM_HEADS, MLSTM_DK, MLSTM_DV
    hp, hs = x_prompt, x_sample
    conv_p_l, C_p_l, n_p_l, m_p_l = [], [], [], []
    conv_s_l, C_s_l, n_s_l, m_s_l = [], [], [], []
    for l in range(DEPTH):
        layer_w = (w_in[l], b_i[l], b_f[l], w_dw[l], b_dw[l], conv_ln_g[l], conv_ln_b[l],
                   mlstm_norm_g[l], w_out[l])
        buf0 = jnp.zeros((B, CONV_KERNEL - 1, CONV_WIDTH), hp.dtype)
        C0 = jnp.zeros((B, H, DK, DV), jnp.float32)
        n0 = jnp.zeros((B, H, DK), jnp.float32)
        m0 = jnp.zeros((B, H), jnp.float32)
        mp, cbp, Cp, np_, mp_ = mixer(rmsnorm(hp, g_mix[l]), buf0, C0, n0, m0, *layer_w)
        ms, cbs, Cs, ns_, ms_ = mixer(rmsnorm(hs, g_mix[l]), state_conv[l], state_C[l],
                                      state_n[l], state_m[l], *layer_w)
        hp = hp + mp
        hs = hs + ms
        tok = jnp.concatenate([rmsnorm(hp, g_ffn[l]).reshape(B * S, D),
                               rmsnorm(hs, g_ffn[l]).reshape(Bd * Sd, D)], axis=0)
        ff = moe(tok, w_router[l], b_router[l], w_gate_up[l], b_gate_up[l], w_down[l], b_down[l])
        hp = hp + ff[:B * S].reshape(B, S, D)
        hs = hs + ff[B * S:].reshape(Bd, Sd, D)
        conv_p_l.append(cbp); C_p_l.append(Cp); n_p_l.append(np_); m_p_l.append(mp_)
        conv_s_l.append(cbs); C_s_l.append(Cs); n_s_l.append(ns_); m_s_l.append(ms_)
    y_prompt = rmsnorm(hp, g_final)
    y_sample = rmsnorm(hs, g_final)
    conv_prompt = jnp.stack(conv_p_l)
    C_prompt = jnp.stack(C_p_l)
    n_prompt = jnp.stack(n_p_l)
    m_prompt = jnp.stack(m_p_l)
    conv_sample = jnp.stack(conv_s_l)
    C_sample = jnp.stack(C_s_l)
    n_sample = jnp.stack(n_s_l)
    m_sample = jnp.stack(m_s_l)
    return (y_prompt, y_sample, conv_prompt, C_prompt, n_prompt, m_prompt,
            conv_sample, C_sample, n_sample, m_sample)
```

```python
import functools

import jax
import jax.numpy as jnp
from jax import lax
from jax.experimental import pallas as pl
from jax.experimental.pallas import tpu as pltpu

F32 = jnp.float32
BF16 = jnp.bfloat16
I32 = jnp.int32

NORM_EPS = 1e-5
TOP_K = 4
SWIGLU_LIMIT = 7.0
SWIGLU_ALPHA = 1.702
LANES = 128
SUBLANES = 8
VMEM_LIMIT_BYTES = 56 * 1024 * 1024
MOE_TILE = 256
NEG_BIG = -1e30


def _round_up(a, b):
    return (a + b - 1) // b * b


def _cparams(sem):
    return pltpu.CompilerParams(dimension_semantics=sem, vmem_limit_bytes=VMEM_LIMIT_BYTES)


def _sigmoid(x):
    return 1.0 / (1.0 + jnp.exp(-x))


def _log_sigmoid(x):
    return jnp.minimum(x, 0.0) - jnp.log(1.0 + jnp.exp(-jnp.abs(x)))


def _norm_gates_kernel(x_ref, g_ref, wg_ref, bg_ref, xn_ref, gt_ref):
    x = x_ref[...]
    y = x * lax.rsqrt(jnp.mean(x * x, axis=-1, keepdims=True) + NORM_EPS) * g_ref[...]
    yb = y.astype(BF16)
    xn_ref[...] = yb
    gt_ref[...] = jnp.dot(yb, wg_ref[...].astype(BF16), preferred_element_type=F32) + bg_ref[...]


def _norm_gates(x, g, wg, bg):
    tn, d = x.shape
    tm = min(512, tn)
    return pl.pallas_call(
        _norm_gates_kernel,
        grid=(tn // tm,),
        in_specs=[pl.BlockSpec((tm, d), lambda i: (i, 0)),
                  pl.BlockSpec((1, d), lambda i: (0, 0)),
                  pl.BlockSpec((d, LANES), lambda i: (0, 0)),
                  pl.BlockSpec((1, LANES), lambda i: (0, 0))],
        out_specs=[pl.BlockSpec((tm, d), lambda i: (i, 0)),
                   pl.BlockSpec((tm, LANES), lambda i: (i, 0))],
        out_shape=[jax.ShapeDtypeStruct((tn, d), BF16), jax.ShapeDtypeStruct((tn, LANES), F32)],
        compiler_params=_cparams(("arbitrary",)),
        name="norm_gates",
    )(x, g, wg, bg)


def _proj_kernel(*refs, n_lhs, has_res):
    lhs_refs = refs[:n_lhs]
    w_ref = refs[n_lhs]
    res_ref = refs[n_lhs + 1] if has_res else None
    o_ref, wb_ref = refs[-2], refs[-1]

    @pl.when(pl.program_id(1) == 0)
    def _():
        wb_ref[...] = w_ref[...].astype(BF16)

    acc = None
    k0 = 0
    for r in lhs_refs:
        kk = r.shape[1]
        part = jnp.dot(r[...], wb_ref[k0:k0 + kk, :], preferred_element_type=F32)
        acc = part if acc is None else acc + part
        k0 += kk
    if has_res:
        acc = acc + res_ref[...]
    o_ref[...] = acc


def _proj(lhs_list, w, n_out, res=None, tn=512):
    tn_rows = lhs_list[0].shape[0]
    k_total = w.shape[0]
    assert sum(a.shape[1] for a in lhs_list) == k_total and n_out % tn == 0
    tm = min(1024, tn_rows)
    in_specs = [pl.BlockSpec((tm, a.shape[1]), lambda j, i: (i, 0)) for a in lhs_list]
    in_specs.append(pl.BlockSpec((k_total, tn), lambda j, i: (0, j)))
    args = list(lhs_list) + [w]
    if res is not None:
        in_specs.append(pl.BlockSpec((tm, tn), lambda j, i: (i, j)))
        args.append(res)
    return pl.pallas_call(
        functools.partial(_proj_kernel, n_lhs=len(lhs_list), has_res=res is not None),
        grid=(n_out // tn, tn_rows // tm),
        in_specs=in_specs,
        out_specs=pl.BlockSpec((tm, tn), lambda j, i: (i, j)),
        out_shape=jax.ShapeDtypeStruct((tn_rows, n_out), F32),
        scratch_shapes=[pltpu.VMEM((k_total, tn), BF16)],
        compiler_params=_cparams(("arbitrary", "arbitrary")),
        name="proj",
    )(*args)


def _conv_kernel(*refs, tl, kw, cc, has_state):
    if has_state:
        ua_ref, ub_ref, buf0_ref, wdw_ref, bdw_ref, lng_ref, lnb_ref, out_ref, nbuf_ref, full_ref, y_ref = refs
    else:
        ua_ref, ub_ref, wdw_ref, bdw_ref, lng_ref, lnb_ref, out_ref, nbuf_ref, full_ref, y_ref = refs
    halo = kw - 1
    padh = _round_up(halo, SUBLANES)
    off = padh - halo
    c = ua_ref.shape[-1]
    l = pl.program_id(1)

    @pl.when(l == 0)
    def _():
        if has_state:
            full_ref[off:padh, :] = buf0_ref[0]
        else:
            full_ref[off:padh, :] = jnp.zeros((halo, c), F32)

    full_ref[padh:padh + tl, :] = ua_ref[0] * _sigmoid(ub_ref[0])

    for c0 in range(0, c, cc):
        acc = jnp.zeros((tl, cc), F32) + bdw_ref[:, c0:c0 + cc]
        for w in range(kw):
            acc = acc + full_ref[off + w:off + w + tl, c0:c0 + cc] * wdw_ref[w:w + 1, c0:c0 + cc]
        y_ref[:, c0:c0 + cc] = acc

    y = y_ref[...]
    mu = jnp.mean(y, axis=-1, keepdims=True)
    yc = y - mu
    var = jnp.mean(yc * yc, axis=-1, keepdims=True)
    z = yc * lax.rsqrt(var + NORM_EPS) * lng_ref[...] + lnb_ref[...]
    out_ref[0] = (z * _sigmoid(z)).astype(BF16)

    tail = full_ref[tl + off:tl + padh, :]

    @pl.when(l == pl.num_programs(1) - 1)
    def _():
        nbuf_ref[0] = tail

    full_ref[off:padh, :] = tail


def _conv_module(u3, buf0, w_dw, b_dw, ln_g, ln_b, c):
    b, l, _ = u3.shape
    kw = w_dw.shape[0]
    halo = kw - 1
    tl = 64 if l % 64 == 0 else l
    padh = _round_up(halo, SUBLANES)
    has_state = buf0 is not None
    in_specs = [pl.BlockSpec((1, tl, c), lambda bi, li: (bi, li, 0)),
                pl.BlockSpec((1, tl, c), lambda bi, li: (bi, li, 1))]
    args = [u3, u3]
    if has_state:
        in_specs.append(pl.BlockSpec((1, halo, c), lambda bi, li: (bi, 0, 0)))
        args.append(buf0)
    in_specs += [pl.BlockSpec((kw, c), lambda bi, li: (0, 0))] + [pl.BlockSpec((1, c), lambda bi, li: (0, 0))] * 3
    args += [w_dw, b_dw.reshape(1, c), ln_g.reshape(1, c), ln_b.reshape(1, c)]
    return pl.pallas_call(
        functools.partial(_conv_kernel, tl=tl, kw=kw, cc=min(256, c), has_state=has_state),
        grid=(b, l // tl),
        in_specs=in_specs,
        out_specs=[pl.BlockSpec((1, tl, c), lambda bi, li: (bi, li, 0)),
                   pl.BlockSpec((1, halo, c), lambda bi, li: (bi, 0, 0))],
        out_shape=[jax.ShapeDtypeStruct((b, l, c), BF16), jax.ShapeDtypeStruct((b, halo, c), F32)],
        scratch_shapes=[pltpu.VMEM((_round_up(padh + tl, SUBLANES), c), F32), pltpu.VMEM((tl, c), F32)],
        compiler_params=_cparams(("arbitrary", "arbitrary")),
        name="conv_module",
    )(*args)


def _mlstm_kernel(*refs, lc, dk, n_heads, has_state):
    if has_state:
        (q_ref, k_ref, v_ref, o_ref, gc_ref, gr_ref, ng_ref, c0_ref, n0_ref, m0_ref,
         hm_ref, co_ref, no_ref, mo_ref, c_s, n_s, m_s) = refs
    else:
        (q_ref, k_ref, v_ref, o_ref, gc_ref, gr_ref, ng_ref,
         hm_ref, co_ref, no_ref, mo_ref, c_s, n_s, m_s) = refs
    hid = pl.program_id(1)
    j = pl.program_id(2)

    @pl.when(j == 0)
    def _():
        if has_state:
            c_s[...] = c0_ref[0, 0]
            n_s[...] = n0_ref[0, 0]
            m_s[...] = m0_ref[0, 0]
        else:
            c_s[...] = jnp.zeros(c_s.shape, F32)
            n_s[...] = jnp.zeros(n_s.shape, F32)
            m_s[...] = jnp.zeros(m_s.shape, F32)

    q = q_ref[0] * (dk ** -0.5)
    kf = k_ref[0]
    qb = q.astype(BF16)
    kb = kf.astype(BF16)
    vb = v_ref[0].astype(BF16)

    gc = gc_ref[0]
    lane = lax.broadcasted_iota(I32, gc.shape, 1)
    gi_col = jnp.sum(jnp.where(lane == hid, gc, 0.0), axis=1, keepdims=True)
    gf_col = jnp.sum(jnp.where(lane == hid + n_heads, gc, 0.0), axis=1, keepdims=True)
    gr = gr_ref[0]
    sub = lax.broadcasted_iota(I32, gr.shape, 0)
    gi_row = jnp.sum(jnp.where(sub == hid, gr, 0.0), axis=0, keepdims=True)
    gf_row = jnp.sum(jnp.where(sub == hid + n_heads, gr, 0.0), axis=0, keepdims=True)
    lf_col = _log_sigmoid(gf_col)
    lf_row = _log_sigmoid(gf_row)

    ri = lax.broadcasted_iota(I32, (lc, lc), 0)
    ci = lax.broadcasted_iota(I32, (lc, lc), 1)
    causal = ri >= ci
    a_col = jnp.sum(jnp.where(causal, lf_row, 0.0), axis=1, keepdims=True)
    a_row = jnp.sum(jnp.where(ri <= ci, lf_col, 0.0), axis=0, keepdims=True)

    m_prev = m_s[...]
    d_log = jnp.where(causal, a_col - a_row + gi_row, -jnp.inf)
    inter = a_col + m_prev
    m_row = jnp.maximum(inter, jnp.max(d_log, axis=1, keepdims=True))
    w_intra = jnp.exp(d_log - m_row)
    w_inter = jnp.exp(inter - m_row)

    s = lax.dot_general(qb, kb, (((1,), (1,)), ((), ())), preferred_element_type=F32) * w_intra
    c_prev = c_s[...]
    n_prev = n_s[...]
    q_c = jnp.dot(qb, c_prev.astype(BF16), preferred_element_type=F32)
    num = jnp.dot(s.astype(BF16), vb, preferred_element_type=F32) + w_inter * q_c
    q_n = jnp.sum(q * n_prev, axis=1, keepdims=True)
    den_raw = jnp.sum(s, axis=1, keepdims=True) + w_inter * q_n
    den = jnp.maximum(jnp.abs(den_raw), jnp.exp(-m_row))
    hh = num / den

    hn = hh * lax.rsqrt(jnp.mean(hh * hh, axis=1, keepdims=True) + NORM_EPS) * ng_ref[0]
    hm_ref[0] = (hn * _sigmoid(o_ref[0])).astype(BF16)

    a_end = a_col[lc - 1:lc, :]
    m_new = m_row[lc - 1:lc, :]
    decay = jnp.exp(a_end + m_prev - m_new)
    w_end = jnp.exp(a_end - a_col + gi_col - m_new)
    kw = kf * w_end
    c_new = decay * c_prev + lax.dot_general(kw.astype(BF16), vb, (((0,), (0,)), ((), ())),
                                             preferred_element_type=F32)
    n_new = decay * n_prev + jnp.sum(kw, axis=0, keepdims=True)
    c_s[...] = c_new
    n_s[...] = n_new
    m_s[...] = m_new

    @pl.when(j == pl.num_programs(2) - 1)
    def _():
        co_ref[0, 0] = c_new
        no_ref[0, 0] = n_new
        mo_ref[0, 0] = m_new


def _mlstm(u3, gates3, norm_g, state, n_heads, dk, dv, c_conv):
    b, l, _ = u3.shape
    lc = 256 if l % 256 == 0 else l
    nc = l // lc
    q0 = 2 * c_conv // dk
    k0 = q0 + n_heads
    v0 = (2 * c_conv + 2 * n_heads * dk) // dv
    o0 = v0 + n_heads
    n_gate_rows = _round_up(2 * n_heads, SUBLANES)
    gates_row = jnp.transpose(gates3[:, :, :n_gate_rows], (0, 2, 1))
    has_state = state is not None
    in_specs = [pl.BlockSpec((1, lc, dk), lambda bi, hi, ji: (bi, ji, q0 + hi)),
                pl.BlockSpec((1, lc, dk), lambda bi, hi, ji: (bi, ji, k0 + hi)),
                pl.BlockSpec((1, lc, dv), lambda bi, hi, ji: (bi, ji, v0 + hi)),
                pl.BlockSpec((1, lc, dv), lambda bi, hi, ji: (bi, ji, o0 + hi)),
                pl.BlockSpec((1, lc, LANES), lambda bi, hi, ji: (bi, ji, 0)),
                pl.BlockSpec((1, n_gate_rows, lc), lambda bi, hi, ji: (bi, 0, ji)),
                pl.BlockSpec((1, 1, dv), lambda bi, hi, ji: (hi, 0, 0))]
    args = [u3, u3, u3, u3, gates3, gates_row, norm_g.reshape(n_heads, 1, dv)]
    if has_state:
        c0, n0, m0 = state
        in_specs += [pl.BlockSpec((1, 1, dk, dv), lambda bi, hi, ji: (bi, hi, 0, 0)),
                     pl.BlockSpec((1, 1, 1, dk), lambda bi, hi, ji: (bi, hi, 0, 0)),
                     pl.BlockSpec((1, 1, 1, 1), lambda bi, hi, ji: (bi, hi, 0, 0))]
        args += [c0, n0.reshape(b, n_heads, 1, dk), m0.reshape(b, n_heads, 1, 1)]
    hm, c_f, n_f, m_f = pl.pallas_call(
        functools.partial(_mlstm_kernel, lc=lc, dk=dk, n_heads=n_heads, has_state=has_state),
        grid=(b, n_heads, nc),
        in_specs=in_specs,
        out_specs=[pl.BlockSpec((1, lc, dv), lambda bi, hi, ji: (bi, ji, hi)),
                   pl.BlockSpec((1, 1, dk, dv), lambda bi, hi, ji: (bi, hi, 0, 0)),
                   pl.BlockSpec((1, 1, 1, dk), lambda bi, hi, ji: (bi, hi, 0, 0)),
                   pl.BlockSpec((1, 1, 1, 1), lambda bi, hi, ji: (bi, hi, 0, 0))],
        out_shape=[jax.ShapeDtypeStruct((b, l, n_heads * dv), BF16),
                   jax.ShapeDtypeStruct((b, n_heads, dk, dv), F32),
                   jax.ShapeDtypeStruct((b, n_heads, 1, dk), F32),
                   jax.ShapeDtypeStruct((b, n_heads, 1, 1), F32)],
        scratch_shapes=[pltpu.VMEM((dk, dv), F32), pltpu.VMEM((1, dk), F32), pltpu.VMEM((1, 1), F32)],
        compiler_params=_cparams(("arbitrary", "arbitrary", "arbitrary")),
        name="mlstm",
    )(*args)
    return hm, c_f, n_f.reshape(b, n_heads, dk), m_f.reshape(b, n_heads)


def _router_kernel(h_ref, g_ref, wr_ref, br_ref, tok_ref, idx_ref, gate_ref, rank_ref, cnt_ref, carry_ref):
    i = pl.program_id(0)

    @pl.when(i == 0)
    def _():
        carry_ref[...] = jnp.zeros(carry_ref.shape, F32)

    x = h_ref[...]
    tok = x * lax.rsqrt(jnp.mean(x * x, axis=-1, keepdims=True) + NORM_EPS) * g_ref[...]
    tok_ref[...] = tok
    lg = jnp.dot(tok, wr_ref[...], preferred_element_type=F32, precision=lax.Precision.HIGHEST) + br_ref[...]
    tm = lg.shape[0]
    lane = lax.broadcasted_iota(I32, lg.shape, 1).astype(F32)
    vals, idxs = [], []
    for _ in range(TOP_K):
        mx = jnp.max(lg, axis=1, keepdims=True)
        ik = jnp.min(jnp.where(lg == mx, lane, float(LANES)), axis=1, keepdims=True)
        vals.append(mx)
        idxs.append(ik)
        lg = jnp.where(lane == ik, -jnp.inf, lg)
    exps = [jnp.exp(v - vals[0]) for v in vals]
    ssum = exps[0]
    for e in exps[1:]:
        ssum = ssum + e
    gates = [e / ssum for e in exps]

    multi = jnp.zeros(lg.shape, F32)
    for ik in idxs:
        multi = multi + (lane == ik).astype(F32)
    ri = lax.broadcasted_iota(I32, (tm, tm), 0)
    ci = lax.broadcasted_iota(I32, (tm, tm), 1)
    below = (ri > ci).astype(BF16)
    cum = jnp.dot(below, multi.astype(BF16), preferred_element_type=F32) + carry_ref[...]
    ranks = [jnp.sum(jnp.where(lane == ik, cum, 0.0), axis=1, keepdims=True) for ik in idxs]
    carry = carry_ref[...] + jnp.sum(multi, axis=0, keepdims=True)
    carry_ref[...] = carry
    cnt_ref[...] = carry.astype(I32)

    idx_o = jnp.zeros(lg.shape, F32)
    gate_o = jnp.zeros(lg.shape, F32)
    rank_o = jnp.zeros(lg.shape, F32)
    for kk in range(TOP_K):
        sel = lane == float(kk)
        idx_o = jnp.where(sel, idxs[kk], idx_o)
        gate_o = jnp.where(sel, gates[kk], gate_o)
        rank_o = jnp.where(sel, ranks[kk], rank_o)
    idx_ref[...] = idx_o.astype(I32)
    gate_ref[...] = gate_o
    rank_ref[...] = rank_o.astype(I32)


def _router(h, g, wr, br):
    t, d = h.shape
    tm = 512 if t % 512 == 0 else t
    blk = lambda w: pl.BlockSpec((tm, w), lambda i: (i, 0))
    return pl.pallas_call(
        _router_kernel,
        grid=(t // tm,),
        in_specs=[blk(d), pl.BlockSpec((1, d), lambda i: (0, 0)),
                  pl.BlockSpec((d, LANES), lambda i: (0, 0)), pl.BlockSpec((1, LANES), lambda i: (0, 0))],
        out_specs=[blk(d), blk(LANES), blk(LANES), blk(LANES), pl.BlockSpec((1, LANES), lambda i: (0, 0))],
        out_shape=[jax.ShapeDtypeStruct((t, d), F32), jax.ShapeDtypeStruct((t, LANES), I32),
                   jax.ShapeDtypeStruct((t, LANES), F32), jax.ShapeDtypeStruct((t, LANES), I32),
                   jax.ShapeDtypeStruct((1, LANES), I32)],
        scratch_shapes=[pltpu.VMEM((1, LANES), F32)],
        compiler_params=_cparams(("arbitrary",)),
        name="router",
    )(h, g, wr, br)


def _gather_rows_kernel(src_ref, tok_hbm, out_ref, buf, sem, *, tm):
    i = pl.program_id(0)
    n = pl.num_programs(0)

    def issue(tile, slot):
        def one(r, carry):
            t = src_ref[tile * tm + r]
            pltpu.make_async_copy(tok_hbm.at[pl.ds(t, 1), :], buf.at[slot, pl.ds(r, 1), :], sem.at[slot]).start()
            return carry
        lax.fori_loop(0, tm, one, 0)

    @pl.when(i == 0)
    def _():
        issue(0, 0)

    @pl.when(i + 1 < n)
    def _():
        issue(i + 1, (i + 1) % 2)

    slot = i % 2

    def wait_one(r, carry):
        pltpu.make_async_copy(tok_hbm.at[pl.ds(0, 1), :], buf.at[slot, pl.ds(r, 1), :], sem.at[slot]).wait()
        return carry
    lax.fori_loop(0, tm, wait_one, 0)
    out_ref[...] = buf[slot].astype(BF16)


def _gather_rows(src, tok, n_rows, tm):
    d = tok.shape[1]
    return pl.pallas_call(
        functools.partial(_gather_rows_kernel, tm=tm),
        grid_spec=pltpu.PrefetchScalarGridSpec(
            num_scalar_prefetch=1,
            grid=(n_rows // tm,),
            in_specs=[pl.BlockSpec(memory_space=pl.ANY)],
            out_specs=pl.BlockSpec((tm, d), lambda i, src_ref: (i, 0)),
            scratch_shapes=[pltpu.VMEM((2, tm, d), F32), pltpu.SemaphoreType.DMA((2,))]),
        out_shape=jax.ShapeDtypeStruct((n_rows, d), BF16),
        compiler_params=_cparams(("arbitrary",)),
        name="moe_gather",
    )(src, tok)


def _new_expert(te_ref, i):
    return jnp.logical_or(i == 0, te_ref[i] != te_ref[jnp.maximum(i - 1, 0)])


def _moe_up_kernel(te_ref, nv_ref, x_ref, wg_ref, wu_ref, bg_ref, bu_ref, act_ref, wgb_ref, wub_ref):
    i = pl.program_id(1)

    @pl.when(_new_expert(te_ref, i))
    def _():
        wgb_ref[...] = wg_ref[0].astype(BF16)
        wub_ref[...] = wu_ref[0].astype(BF16)

    @pl.when(i < nv_ref[0])
    def _():
        x = x_ref[...]
        g = jnp.dot(x, wgb_ref[...], preferred_element_type=F32) + bg_ref[0]
        up = jnp.dot(x, wub_ref[...], preferred_element_type=F32) + bu_ref[0]
        g = jnp.minimum(g, SWIGLU_LIMIT)
        up = jnp.clip(up, -SWIGLU_LIMIT, SWIGLU_LIMIT)
        act_ref[...] = ((up + 1.0) * (g * _sigmoid(SWIGLU_ALPHA * g))).astype(BF16)

    @pl.when(i >= nv_ref[0])
    def _():
        act_ref[...] = jnp.zeros(act_ref.shape, BF16)


def _moe_up(tile_e, n_valid, xs, w_gate_up, b_gate_up, tm, tf=512):
    n_rows, d = xs.shape
    n_exp, _, two_f = w_gate_up.shape
    f = two_f // 2
    nj = f // tf
    row_blk = lambda j, i, te, nv: (jnp.minimum(i, nv[0] - 1), 0)
    return pl.pallas_call(
        _moe_up_kernel,
        grid_spec=pltpu.PrefetchScalarGridSpec(
            num_scalar_prefetch=2,
            grid=(nj, n_rows // tm),
            in_specs=[pl.BlockSpec((tm, d), row_blk),
                      pl.BlockSpec((1, d, tf), lambda j, i, te, nv: (te[i], 0, j)),
                      pl.BlockSpec((1, d, tf), lambda j, i, te, nv: (te[i], 0, nj + j)),
                      pl.BlockSpec((1, 1, tf), lambda j, i, te, nv: (te[i], 0, j)),
                      pl.BlockSpec((1, 1, tf), lambda j, i, te, nv: (te[i], 0, nj + j))],
            out_specs=pl.BlockSpec((tm, tf), lambda j, i, te, nv: (i, j)),
            scratch_shapes=[pltpu.VMEM((d, tf), BF16), pltpu.VMEM((d, tf), BF16)]),
        out_shape=jax.ShapeDtypeStruct((n_rows, f), BF16),
        compiler_params=_cparams(("arbitrary", "arbitrary")),
        name="moe_up",
    )(tile_e, n_valid, xs, w_gate_up, w_gate_up, b_gate_up.reshape(n_exp, 1, two_f),
      b_gate_up.reshape(n_exp, 1, two_f))


def _moe_down_kernel(te_ref, nv_ref, a_ref, wd_ref, bd_ref, o_ref, wdb_ref):
    i = pl.program_id(1)

    @pl.when(_new_expert(te_ref, i))
    def _():
        wdb_ref[...] = wd_ref[0].astype(BF16)

    @pl.when(i < nv_ref[0])
    def _():
        o_ref[...] = jnp.dot(a_ref[...], wdb_ref[...], preferred_element_type=F32) + bd_ref[0]

    @pl.when(i >= nv_ref[0])
    def _():
        o_ref[...] = jnp.zeros(o_ref.shape, F32)


def _moe_down(tile_e, n_valid, act, w_down, b_down, tm, tn=512):
    n_rows, f = act.shape
    n_exp, _, d = w_down.shape
    return pl.pallas_call(
        _moe_down_kernel,
        grid_spec=pltpu.PrefetchScalarGridSpec(
            num_scalar_prefetch=2,
            grid=(d // tn, n_rows // tm),
            in_specs=[pl.BlockSpec((tm, f), lambda j, i, te, nv: (jnp.minimum(i, nv[0] - 1), 0)),
                      pl.BlockSpec((1, f, tn), lambda j, i, te, nv: (te[i], 0, j)),
                      pl.BlockSpec((1, 1, tn), lambda j, i, te, nv: (te[i], 0, j))],
            out_specs=pl.BlockSpec((tm, tn), lambda j, i, te, nv: (i, j)),
            scratch_shapes=[pltpu.VMEM((f, tn), BF16)]),
        out_shape=jax.ShapeDtypeStruct((n_rows, d), F32),
        compiler_params=_cparams(("arbitrary", "arbitrary")),
        name="moe_down",
    )(tile_e, n_valid, act, w_down, b_down.reshape(n_exp, 1, d))


def _combine_kernel(dest_ref, o_hbm, gate_ref, h_ref, g_ref, y_ref, buf, sem, *, tm):
    i = pl.program_id(0)
    n = pl.num_programs(0)

    def issue(tile, slot):
        def one(r, carry):
            for kk in range(TOP_K):
                p = dest_ref[(tile * tm + r) * TOP_K + kk]
                pltpu.make_async_copy(o_hbm.at[pl.ds(p, 1), :], buf.at[slot, kk, pl.ds(r, 1), :],
                                      sem.at[slot]).start()
            return carry
        lax.fori_loop(0, tm, one, 0)

    @pl.when(i == 0)
    def _():
        issue(0, 0)

    @pl.when(i + 1 < n)
    def _():
        issue(i + 1, (i + 1) % 2)

    slot = i % 2

    def wait_one(r, carry):
        for kk in range(TOP_K):
            pltpu.make_async_copy(o_hbm.at[pl.ds(0, 1), :], buf.at[slot, kk, pl.ds(r, 1), :], sem.at[slot]).wait()
        return carry
    lax.fori_loop(0, tm, wait_one, 0)

    gates = gate_ref[...]
    acc = h_ref[...]
    for kk in range(TOP_K):
        acc = acc + gates[:, kk:kk + 1] * buf[slot, kk]
    y_ref[...] = acc * lax.rsqrt(jnp.mean(acc * acc, axis=-1, keepdims=True) + NORM_EPS) * g_ref[...]


def _combine(dest_flat, o_sorted, gates, h, g_final, tm=128):
    t, d = h.shape
    tm = tm if t % tm == 0 else t
    return pl.pallas_call(
        functools.partial(_combine_kernel, tm=tm),
        grid_spec=pltpu.PrefetchScalarGridSpec(
            num_scalar_prefetch=1,
            grid=(t // tm,),
            in_specs=[pl.BlockSpec(memory_space=pl.ANY),
                      pl.BlockSpec((tm, LANES), lambda i, dref: (i, 0)),
                      pl.BlockSpec((tm, d), lambda i, dref: (i, 0)),
                      pl.BlockSpec((1, d), lambda i, dref: (0, 0))],
            out_specs=pl.BlockSpec((tm, d), lambda i, dref: (i, 0)),
            scratch_shapes=[pltpu.VMEM((2, TOP_K, tm, d), F32), pltpu.SemaphoreType.DMA((2,))]),
        out_shape=jax.ShapeDtypeStruct((t, d), F32),
        compiler_params=_cparams(("arbitrary",)),
        name="moe_combine",
    )(dest_flat, o_sorted, gates, h, g_final.reshape(1, d))


def _mixer(x3, conv_buf, state, g_mix, w_in, b_i, b_f, w_dw, b_dw, ln_g, ln_b, norm_g, w_out,
           n_heads, dk, dv, c_conv):
    b, l, d = x3.shape
    x2 = x3.reshape(b * l, d)
    n_main = 2 * c_conv + 2 * n_heads * dk + 2 * n_heads * dv
    w_gate = jnp.pad(w_in[:, n_main:], ((0, 0), (0, LANES - 2 * n_heads)))
    b_gate = jnp.pad(jnp.concatenate([b_i, b_f]), (0, LANES - 2 * n_heads)).reshape(1, LANES)
    xn, gates = _norm_gates(x2, g_mix.reshape(1, d), w_gate, b_gate)
    u = _proj([xn], w_in, n_main)
    u3 = u.reshape(b, l, n_main)
    conv_out, new_buf = _conv_module(u3, conv_buf, w_dw, b_dw, ln_g, ln_b, c_conv)
    hm, c_f, n_f, m_f = _mlstm(u3, gates.reshape(b, l, LANES), norm_g, state, n_heads, dk, dv, c_conv)
    h_res = _proj([conv_out.reshape(b * l, c_conv), hm.reshape(b * l, n_heads * dv)], w_out, d, res=x2)
    return h_res, new_buf, c_f, n_f, m_f


def _moe(h, g_ffn, w_router, b_router, w_gate_up, b_gate_up, w_down, b_down, g_final):
    t, d = h.shape
    n_exp = w_router.shape[1]
    tm = MOE_TILE
    wr = jnp.pad(w_router, ((0, 0), (0, LANES - n_exp)))
    br = jnp.pad(b_router, (0, LANES - n_exp), constant_values=NEG_BIG).reshape(1, LANES)
    tok, idx_o, gate_o, rank_o, cnt_o = _router(h, g_ffn.reshape(1, d), wr, br)

    n_tiles = (t * TOP_K + n_exp * (tm - 1)) // tm
    n_rows = n_tiles * tm
    idx = idx_o[:, :TOP_K]
    counts = cnt_o[0, :n_exp]
    padded = (counts + tm - 1) // tm * tm
    ends = jnp.cumsum(padded)
    starts = ends - padded
    dest = (starts[idx] + rank_o[:, :TOP_K]).reshape(-1)
    token_of_pair = jnp.repeat(jnp.arange(t, dtype=I32), TOP_K)
    src = jnp.zeros((n_rows,), I32).at[dest].set(token_of_pair)
    tile_e = jnp.minimum(jnp.searchsorted(ends, jnp.arange(n_tiles, dtype=I32) * tm, side="right"),
                         n_exp - 1).astype(I32)
    n_valid = (ends[-1:] // tm).astype(I32)

    xs = _gather_rows(src, tok, n_rows, tm)
    act = _moe_up(tile_e, n_valid, xs, w_gate_up, b_gate_up, tm)
    o_sorted = _moe_down(tile_e, n_valid, act, w_down, b_down, tm)
    return _combine(dest, o_sorted, gate_o, h, g_final)


def kernel(x_prompt, x_sample, state_conv, state_C, state_n, state_m, g_mix, w_in, b_i, b_f, w_dw, b_dw,
           conv_ln_g, conv_ln_b, mlstm_norm_g, w_out, g_ffn, w_router, b_router, w_gate_up, b_gate_up,
           w_down, b_down, g_final):
    b, s, d = x_prompt.shape
    bd, sd, _ = x_sample.shape
    depth = w_in.shape[0]
    assert depth == 1, "the final norm is fused into the MoE combine of the single layer"
    n_heads, dk, dv = state_C.shape[2], state_C.shape[3], state_C.shape[4]
    c_conv = state_conv.shape[3]
    l = 0
    layer_w = (g_mix[l], w_in[l], b_i[l], b_f[l], w_dw[l], b_dw[l], conv_ln_g[l], conv_ln_b[l],
               mlstm_norm_g[l], w_out[l])
    hp, cbp, c_p, n_p, m_p = _mixer(x_prompt, None, None, *layer_w, n_heads, dk, dv, c_conv)
    hs, cbs, c_s, n_s, m_s = _mixer(x_sample, state_conv[l], (state_C[l], state_n[l], state_m[l]),
                                    *layer_w, n_heads, dk, dv, c_conv)
    h_all = jnp.concatenate([hp, hs], axis=0)
    y = _moe(h_all, g_ffn[l], w_router[l], b_router[l], w_gate_up[l], b_gate_up[l], w_down[l], b_down[l],
             g_final)
    y_prompt = y[:b * s].reshape(b, s, d)
    y_sample = y[b * s:].reshape(bd, sd, d)
    return (y_prompt, y_sample, cbp[None], c_p[None], n_p[None], m_p[None],
            cbs[None], c_s[None], n_s[None], m_s[None])
```

```python
import functools

import jax
import jax.numpy as jnp
from jax import lax
from jax.experimental import pallas as pl
from jax.experimental.pallas import tpu as pltpu

F32 = jnp.float32
BF16 = jnp.bfloat16
I32 = jnp.int32

NORM_EPS = 1e-5
TOP_K = 4
SWIGLU_LIMIT = 7.0
SWIGLU_ALPHA = 1.702
LANES = 128
SUBLANES = 8
VMEM_LIMIT_BYTES = 56 * 1024 * 1024
NEG_BIG = -1e30

MOE_PIECE = 128
MOE_CAP_PIECES = 16
MOE_TILE_PIECES = (8, 2, 1)
GATHER_TILE = 256


def _round_up(a, b):
    return (a + b - 1) // b * b


def _cparams(sem):
    return pltpu.CompilerParams(dimension_semantics=sem, vmem_limit_bytes=VMEM_LIMIT_BYTES)


def _sigmoid(x):
    return 1.0 / (1.0 + jnp.exp(-x))


def _log_sigmoid(x):
    return jnp.minimum(x, 0.0) - jnp.log(1.0 + jnp.exp(-jnp.abs(x)))


def _rms(x, g):
    return x * lax.rsqrt(jnp.mean(x * x, axis=-1, keepdims=True) + NORM_EPS) * g


_NT = (((1,), (1,)), ((), ()))


def _norm_gates_kernel(x_ref, g_ref, wgt_ref, bg_ref, xn_ref, gt_ref):
    yb = _rms(x_ref[...], g_ref[...]).astype(BF16)
    xn_ref[...] = yb
    gt_ref[...] = lax.dot_general(yb, wgt_ref[...].astype(BF16), _NT, preferred_element_type=F32) + bg_ref[...]


def _norm_gates(x, g, wgt, bg):
    tn, d = x.shape
    tm = min(512, tn)
    return pl.pallas_call(
        _norm_gates_kernel,
        grid=(tn // tm,),
        in_specs=[pl.BlockSpec((tm, d), lambda i: (i, 0)),
                  pl.BlockSpec((1, d), lambda i: (0, 0)),
                  pl.BlockSpec((LANES, d), lambda i: (0, 0)),
                  pl.BlockSpec((1, LANES), lambda i: (0, 0))],
        out_specs=[pl.BlockSpec((tm, d), lambda i: (i, 0)),
                   pl.BlockSpec((tm, LANES), lambda i: (i, 0))],
        out_shape=[jax.ShapeDtypeStruct((tn, d), BF16), jax.ShapeDtypeStruct((tn, LANES), F32)],
        compiler_params=_cparams(("arbitrary",)),
        name="norm_gates",
    )(x, g, wgt, bg)


def _proj_kernel(*refs, n_lhs, has_res, w_is_nk):
    lhs_refs = refs[:n_lhs]
    w_ref = refs[n_lhs]
    res_ref = refs[n_lhs + 1] if has_res else None
    o_ref = refs[-1]
    acc = None
    k0 = 0
    for r in lhs_refs:
        kk = r.shape[1]
        lhs = r[...].astype(BF16)
        if w_is_nk:
            part = lax.dot_general(lhs, w_ref[:, k0:k0 + kk].astype(BF16), _NT, preferred_element_type=F32)
        else:
            part = jnp.dot(lhs, w_ref[k0:k0 + kk, :].astype(BF16), preferred_element_type=F32)
        acc = part if acc is None else acc + part
        k0 += kk
    if has_res:
        acc = acc + res_ref[...]
    o_ref[...] = acc


def _proj(lhs_list, w, n_out, res=None, w_is_nk=False, tn=512):
    rows = lhs_list[0].shape[0]
    k_total = w.shape[1] if w_is_nk else w.shape[0]
    assert sum(a.shape[1] for a in lhs_list) == k_total and n_out % tn == 0
    tm = min(1024, rows)
    in_specs = [pl.BlockSpec((tm, a.shape[1]), lambda j, i: (i, 0)) for a in lhs_list]
    if w_is_nk:
        in_specs.append(pl.BlockSpec((tn, k_total), lambda j, i: (j, 0)))
    else:
        in_specs.append(pl.BlockSpec((k_total, tn), lambda j, i: (0, j)))
    args = list(lhs_list) + [w]
    if res is not None:
        in_specs.append(pl.BlockSpec((tm, tn), lambda j, i: (i, j)))
        args.append(res)
    return pl.pallas_call(
        functools.partial(_proj_kernel, n_lhs=len(lhs_list), has_res=res is not None, w_is_nk=w_is_nk),
        grid=(n_out // tn, rows // tm),
        in_specs=in_specs,
        out_specs=pl.BlockSpec((tm, tn), lambda j, i: (i, j)),
        out_shape=jax.ShapeDtypeStruct((rows, n_out), F32),
        compiler_params=_cparams(("arbitrary", "arbitrary")),
        name="proj",
    )(*args)


def _layernorm_swish(y, g, b):
    mu = jnp.mean(y, axis=-1, keepdims=True)
    yc = y - mu
    var = jnp.mean(yc * yc, axis=-1, keepdims=True)
    z = yc * lax.rsqrt(var + NORM_EPS) * g + b
    return z * _sigmoid(z)


def _conv_seq_kernel(ua_ref, ub_ref, wdw_ref, bdw_ref, lng_ref, lnb_ref, out_ref, nbuf_ref, full_ref, y_ref,
                     *, tl, kw, cc):
    halo = kw - 1
    padh = _round_up(halo, SUBLANES)
    off = padh - halo
    c = ua_ref.shape[-1]
    l = pl.program_id(1)

    @pl.when(l == 0)
    def _():
        full_ref[off:padh, :] = jnp.zeros((halo, c), F32)

    full_ref[padh:padh + tl, :] = ua_ref[0] * _sigmoid(ub_ref[0])

    for c0 in range(0, c, cc):
        acc = jnp.zeros((tl, cc), F32) + bdw_ref[:, c0:c0 + cc]
        for w in range(kw):
            acc = acc + full_ref[off + w:off + w + tl, c0:c0 + cc] * wdw_ref[w:w + 1, c0:c0 + cc]
        y_ref[:, c0:c0 + cc] = acc

    out_ref[0] = _layernorm_swish(y_ref[...], lng_ref[...], lnb_ref[...]).astype(BF16)

    tail = full_ref[tl + off:tl + padh, :]

    @pl.when(l == pl.num_programs(1) - 1)
    def _():
        nbuf_ref[0] = tail

    full_ref[off:padh, :] = tail


def _conv_seq(u3, w_dw, b_dw, ln_g, ln_b, c):
    b, l, _ = u3.shape
    kw = w_dw.shape[0]
    halo = kw - 1
    tl = 64
    assert l % tl == 0
    padh = _round_up(halo, SUBLANES)
    return pl.pallas_call(
        functools.partial(_conv_seq_kernel, tl=tl, kw=kw, cc=min(256, c)),
        grid=(b, l // tl),
        in_specs=[pl.BlockSpec((1, tl, c), lambda bi, li: (bi, li, 0)),
                  pl.BlockSpec((1, tl, c), lambda bi, li: (bi, li, 1)),
                  pl.BlockSpec((kw, c), lambda bi, li: (0, 0))] + [pl.BlockSpec((1, c), lambda bi, li: (0, 0))] * 3,
        out_specs=[pl.BlockSpec((1, tl, c), lambda bi, li: (bi, li, 0)),
                   pl.BlockSpec((1, halo, c), lambda bi, li: (bi, 0, 0))],
        out_shape=[jax.ShapeDtypeStruct((b, l, c), BF16), jax.ShapeDtypeStruct((b, halo, c), F32)],
        scratch_shapes=[pltpu.VMEM((padh + tl, c), F32), pltpu.VMEM((tl, c), F32)],
        compiler_params=_cparams(("arbitrary", "arbitrary")),
        name="conv_seq",
    )(u3, u3, w_dw, b_dw.reshape(1, c), ln_g.reshape(1, c), ln_b.reshape(1, c))


def _conv_step_kernel(ua_ref, ub_ref, st_ref, wdw_ref, bdw_ref, lng_ref, lnb_ref, out_ref, nst_ref, y_ref,
                      *, ls, kw, cc):
    halo = kw - 1
    j = pl.program_id(0)
    glu = [ua_ref[t] * _sigmoid(ub_ref[t]) for t in range(ls)]

    def window(idx):
        return st_ref[idx] if idx < halo else glu[idx - halo]

    for t in range(ls):
        acc = jnp.zeros(glu[0].shape, F32) + bdw_ref[...]
        for w in range(kw):
            acc = acc + window(t + w) * wdw_ref[w:w + 1, :]
        y_ref[j, t] = acc
    for idx in range(halo):
        nst_ref[idx] = window(ls + idx)

    @pl.when(j == pl.num_programs(0) - 1)
    def _():
        for t in range(ls):
            y = jnp.concatenate([y_ref[jj, t] for jj in range(y_ref.shape[0])], axis=-1)
            out_ref[t] = _layernorm_swish(y, lng_ref[...], lnb_ref[...]).astype(BF16)


def _conv_step(ug, state_t, w_dw, b_dw, ln_g, ln_b, c):
    ls, b, _ = ug.shape
    kw = w_dw.shape[0]
    halo = kw - 1
    cc = min(256, c)
    nj = c // cc
    return pl.pallas_call(
        functools.partial(_conv_step_kernel, ls=ls, kw=kw, cc=cc),
        grid=(nj,),
        in_specs=[pl.BlockSpec((ls, b, cc), lambda j: (0, 0, j)),
                  pl.BlockSpec((ls, b, cc), lambda j: (0, 0, nj + j)),
                  pl.BlockSpec((halo, b, cc), lambda j: (0, 0, j)),
                  pl.BlockSpec((kw, cc), lambda j: (0, j)),
                  pl.BlockSpec((1, cc), lambda j: (0, j)),
                  pl.BlockSpec((1, c), lambda j: (0, 0)),
                  pl.BlockSpec((1, c), lambda j: (0, 0))],
        out_specs=[pl.BlockSpec((ls, b, c), lambda j: (0, 0, 0)),
                   pl.BlockSpec((halo, b, cc), lambda j: (0, 0, j))],
        out_shape=[jax.ShapeDtypeStruct((ls, b, c), BF16), jax.ShapeDtypeStruct((halo, b, c), F32)],
        scratch_shapes=[pltpu.VMEM((nj, ls, b, cc), F32)],
        compiler_params=_cparams(("arbitrary",)),
        name="conv_step",
    )(ug, ug, state_t, w_dw, b_dw.reshape(1, c), ln_g.reshape(1, c), ln_b.reshape(1, c))


def _mlstm_kernel(*refs, lc, dk, n_heads, has_state):
    if has_state:
        (q_ref, k_ref, v_ref, o_ref, gc_ref, gr_ref, ng_ref, c0_ref, n0_ref, m0_ref,
         hm_ref, co_ref, no_ref, mo_ref, c_s, n_s, m_s) = refs
    else:
        (q_ref, k_ref, v_ref, o_ref, gc_ref, gr_ref, ng_ref,
         hm_ref, co_ref, no_ref, mo_ref, c_s, n_s, m_s) = refs
    hid = pl.program_id(1)
    j = pl.program_id(2)

    @pl.when(j == 0)
    def _():
        if has_state:
            c_s[...] = c0_ref[0, 0]
            n_s[...] = n0_ref[0, 0]
            m_s[...] = m0_ref[0, 0]
        else:
            c_s[...] = jnp.zeros(c_s.shape, F32)
            n_s[...] = jnp.zeros(n_s.shape, F32)
            m_s[...] = jnp.zeros(m_s.shape, F32)

    q = q_ref[0] * (dk ** -0.5)
    kf = k_ref[0]
    qb = q.astype(BF16)
    kb = kf.astype(BF16)
    vb = v_ref[0].astype(BF16)

    gc = gc_ref[0]
    lane = lax.broadcasted_iota(I32, gc.shape, 1)
    gi_col = jnp.sum(jnp.where(lane == hid, gc, 0.0), axis=1, keepdims=True)
    gf_col = jnp.sum(jnp.where(lane == hid + n_heads, gc, 0.0), axis=1, keepdims=True)
    gr = gr_ref[0]
    sub = lax.broadcasted_iota(I32, gr.shape, 0)
    gi_row = jnp.sum(jnp.where(sub == hid, gr, 0.0), axis=0, keepdims=True)
    gf_row = jnp.sum(jnp.where(sub == hid + n_heads, gr, 0.0), axis=0, keepdims=True)
    lf_col = _log_sigmoid(gf_col)
    lf_row = _log_sigmoid(gf_row)

    ri = lax.broadcasted_iota(I32, (lc, lc), 0)
    ci = lax.broadcasted_iota(I32, (lc, lc), 1)
    causal = ri >= ci
    a_col = jnp.sum(jnp.where(causal, lf_row, 0.0), axis=1, keepdims=True)
    a_row = jnp.sum(jnp.where(ri <= ci, lf_col, 0.0), axis=0, keepdims=True)

    m_prev = m_s[...]
    d_log = jnp.where(causal, a_col - a_row + gi_row, -jnp.inf)
    inter = a_col + m_prev
    m_row = jnp.maximum(inter, jnp.max(d_log, axis=1, keepdims=True))
    w_intra = jnp.exp(d_log - m_row)
    w_inter = jnp.exp(inter - m_row)

    s = lax.dot_general(qb, kb, _NT, preferred_element_type=F32) * w_intra
    c_prev = c_s[...]
    n_prev = n_s[...]
    q_c = jnp.dot(qb, c_prev.astype(BF16), preferred_element_type=F32)
    num = jnp.dot(s.astype(BF16), vb, preferred_element_type=F32) + w_inter * q_c
    q_n = jnp.sum(q * n_prev, axis=1, keepdims=True)
    den_raw = jnp.sum(s, axis=1, keepdims=True) + w_inter * q_n
    den = jnp.maximum(jnp.abs(den_raw), jnp.exp(-m_row))
    hh = num / den

    hn = _rms(hh, ng_ref[0])
    hm_ref[0] = (hn * _sigmoid(o_ref[0])).astype(BF16)

    a_end = a_col[lc - 1:lc, :]
    m_new = m_row[lc - 1:lc, :]
    decay = jnp.exp(a_end + m_prev - m_new)
    w_end = jnp.exp(a_end - a_col + gi_col - m_new)
    kw = kf * w_end
    c_new = decay * c_prev + lax.dot_general(kw.astype(BF16), vb, (((0,), (0,)), ((), ())),
                                             preferred_element_type=F32)
    n_new = decay * n_prev + jnp.sum(kw, axis=0, keepdims=True)
    c_s[...] = c_new
    n_s[...] = n_new
    m_s[...] = m_new

    @pl.when(j == pl.num_programs(2) - 1)
    def _():
        co_ref[0, 0] = c_new
        no_ref[0, 0] = n_new
        mo_ref[0, 0] = m_new


def _mlstm(u3, gates3, norm_g, state, n_heads, dk, dv, c_conv):
    b, l, _ = u3.shape
    lc = 256 if l % 256 == 0 else l
    nc = l // lc
    q0 = 2 * c_conv // dk
    k0 = q0 + n_heads
    v0 = (2 * c_conv + 2 * n_heads * dk) // dv
    o0 = v0 + n_heads
    n_gate_rows = _round_up(2 * n_heads, SUBLANES)
    gates_row = jnp.transpose(gates3[:, :, :n_gate_rows], (0, 2, 1))
    has_state = state is not None
    in_specs = [pl.BlockSpec((1, lc, dk), lambda bi, hi, ji: (bi, ji, q0 + hi)),
                pl.BlockSpec((1, lc, dk), lambda bi, hi, ji: (bi, ji, k0 + hi)),
                pl.BlockSpec((1, lc, dv), lambda bi, hi, ji: (bi, ji, v0 + hi)),
                pl.BlockSpec((1, lc, dv), lambda bi, hi, ji: (bi, ji, o0 + hi)),
                pl.BlockSpec((1, lc, LANES), lambda bi, hi, ji: (bi, ji, 0)),
                pl.BlockSpec((1, n_gate_rows, lc), lambda bi, hi, ji: (bi, 0, ji)),
                pl.BlockSpec((1, 1, dv), lambda bi, hi, ji: (hi, 0, 0))]
    args = [u3, u3, u3, u3, gates3, gates_row, norm_g.reshape(n_heads, 1, dv)]
    if has_state:
        c0, n0, m0 = state
        in_specs += [pl.BlockSpec((1, 1, dk, dv), lambda bi, hi, ji: (bi, hi, 0, 0)),
                     pl.BlockSpec((1, 1, 1, dk), lambda bi, hi, ji: (bi, hi, 0, 0)),
                     pl.BlockSpec((1, 1, 1, 1), lambda bi, hi, ji: (bi, hi, 0, 0))]
        args += [c0, n0.reshape(b, n_heads, 1, dk), m0.reshape(b, n_heads, 1, 1)]
    hm, c_f, n_f, m_f = pl.pallas_call(
        functools.partial(_mlstm_kernel, lc=lc, dk=dk, n_heads=n_heads, has_state=has_state),
        grid=(b, n_heads, nc),
        in_specs=in_specs,
        out_specs=[pl.BlockSpec((1, lc, dv), lambda bi, hi, ji: (bi, ji, hi)),
                   pl.BlockSpec((1, 1, dk, dv), lambda bi, hi, ji: (bi, hi, 0, 0)),
                   pl.BlockSpec((1, 1, 1, dk), lambda bi, hi, ji: (bi, hi, 0, 0)),
                   pl.BlockSpec((1, 1, 1, 1), lambda bi, hi, ji: (bi, hi, 0, 0))],
        out_shape=[jax.ShapeDtypeStruct((b, l, n_heads * dv), BF16),
                   jax.ShapeDtypeStruct((b, n_heads, dk, dv), F32),
                   jax.ShapeDtypeStruct((b, n_heads, 1, dk), F32),
                   jax.ShapeDtypeStruct((b, n_heads, 1, 1), F32)],
        scratch_shapes=[pltpu.VMEM((dk, dv), F32), pltpu.VMEM((1, dk), F32), pltpu.VMEM((1, 1), F32)],
        compiler_params=_cparams(("arbitrary", "arbitrary", "arbitrary")),
        name="mlstm",
    )(*args)
    return hm, c_f, n_f.reshape(b, n_heads, dk), m_f.reshape(b, n_heads)


def _router_kernel(hp_ref, hs_ref, g_ref, wrt_ref, br_ref, tok_ref, idx_ref, gate_ref, rank_ref, cnt_ref,
                   carry_ref, *, n_prompt_tiles):
    i = pl.program_id(0)

    @pl.when(i == 0)
    def _():
        carry_ref[...] = jnp.zeros(carry_ref.shape, F32)

    x = jnp.where(i < n_prompt_tiles, hp_ref[...], hs_ref[...])
    tok = _rms(x, g_ref[...])
    tok_ref[...] = tok.reshape(tok_ref.shape)
    lg = lax.dot_general(tok, wrt_ref[...], _NT, preferred_element_type=F32,
                         precision=lax.Precision.HIGHEST) + br_ref[...]
    tm = lg.shape[0]
    lane = lax.broadcasted_iota(I32, lg.shape, 1).astype(F32)
    vals, idxs = [], []
    for _ in range(TOP_K):
        mx = jnp.max(lg, axis=1, keepdims=True)
        ik = jnp.min(jnp.where(lg == mx, lane, float(LANES)), axis=1, keepdims=True)
        vals.append(mx)
        idxs.append(ik)
        lg = jnp.where(lane == ik, -jnp.inf, lg)
    exps = [jnp.exp(v - vals[0]) for v in vals]
    ssum = exps[0]
    for e in exps[1:]:
        ssum = ssum + e
    gates = [e / ssum for e in exps]

    multi = jnp.zeros(lg.shape, F32)
    for ik in idxs:
        multi = multi + (lane == ik).astype(F32)
    ri = lax.broadcasted_iota(I32, (tm, tm), 0)
    ci = lax.broadcasted_iota(I32, (tm, tm), 1)
    below = (ri > ci).astype(BF16)
    cum = jnp.dot(below, multi.astype(BF16), preferred_element_type=F32) + carry_ref[...]
    ranks = [jnp.sum(jnp.where(lane == ik, cum, 0.0), axis=1, keepdims=True) for ik in idxs]
    carry = carry_ref[...] + jnp.sum(multi, axis=0, keepdims=True)
    carry_ref[...] = carry
    cnt_ref[...] = carry.astype(I32)

    idx_o = jnp.zeros(lg.shape, F32)
    gate_o = jnp.zeros(lg.shape, F32)
    rank_o = jnp.zeros(lg.shape, F32)
    for kk in range(TOP_K):
        sel = lane == float(kk)
        idx_o = jnp.where(sel, idxs[kk], idx_o)
        gate_o = jnp.where(sel, gates[kk], gate_o)
        rank_o = jnp.where(sel, ranks[kk], rank_o)
    idx_ref[...] = idx_o.astype(I32)
    gate_ref[...] = gate_o
    rank_ref[...] = rank_o.astype(I32)


def _router(hp, hs, g, wrt, br, tm):
    tp, d = hp.shape
    ts = hs.shape[0]
    assert tp % tm == 0 and ts % tm == 0
    npt, nst = tp // tm, ts // tm
    t = tp + ts
    dg = d // LANES
    blk = lambda w: pl.BlockSpec((tm, w), lambda i: (i, 0))
    return pl.pallas_call(
        functools.partial(_router_kernel, n_prompt_tiles=npt),
        grid=(npt + nst,),
        in_specs=[pl.BlockSpec((tm, d), lambda i: (jnp.minimum(i, npt - 1), 0)),
                  pl.BlockSpec((tm, d), lambda i: (jnp.maximum(i - npt, 0), 0)),
                  pl.BlockSpec((1, d), lambda i: (0, 0)),
                  pl.BlockSpec((LANES, d), lambda i: (0, 0)),
                  pl.BlockSpec((1, LANES), lambda i: (0, 0))],
        out_specs=[pl.BlockSpec((tm, dg, LANES), lambda i: (i, 0, 0)), blk(LANES), blk(LANES), blk(LANES),
                   pl.BlockSpec((1, LANES), lambda i: (0, 0))],
        out_shape=[jax.ShapeDtypeStruct((t, dg, LANES), F32), jax.ShapeDtypeStruct((t, LANES), I32),
                   jax.ShapeDtypeStruct((t, LANES), F32), jax.ShapeDtypeStruct((t, LANES), I32),
                   jax.ShapeDtypeStruct((1, LANES), I32)],
        scratch_shapes=[pltpu.VMEM((1, LANES), F32)],
        compiler_params=_cparams(("arbitrary",)),
        name="router",
    )(hp, hs, g, wrt, br)


def _gather_rows_kernel(src_ref, tok_hbm, out_ref, buf, sem, *, tm):
    i = pl.program_id(0)
    n = pl.num_programs(0)

    def tile_copy(slot):
        return pltpu.make_async_copy(tok_hbm.at[pl.ds(0, tm)], buf.at[slot], sem.at[slot])

    def issue(tile, slot):
        def one(r, carry):
            t = src_ref[tile * tm + r]
            pltpu.make_async_copy(tok_hbm.at[t], buf.at[slot, r], sem.at[slot]).start()
            return carry
        lax.fori_loop(0, tm, one, 0, unroll=8)

    @pl.when(i == 0)
    def _():
        issue(0, 0)

    @pl.when(i + 1 < n)
    def _():
        issue(i + 1, (i + 1) % 2)

    slot = i % 2
    tile_copy(slot).wait()
    out_ref[...] = buf[slot].reshape(out_ref.shape).astype(BF16)


def _gather_rows(src, tok3, n_rows, tm):
    _, dg, lanes = tok3.shape
    return pl.pallas_call(
        functools.partial(_gather_rows_kernel, tm=tm),
        grid_spec=pltpu.PrefetchScalarGridSpec(
            num_scalar_prefetch=1,
            grid=(n_rows // tm,),
            in_specs=[pl.BlockSpec(memory_space=pl.ANY)],
            out_specs=pl.BlockSpec((tm, dg * lanes), lambda i, src_ref: (i, 0)),
            scratch_shapes=[pltpu.VMEM((2, tm, dg, lanes), F32), pltpu.SemaphoreType.DMA((2,))]),
        out_shape=jax.ShapeDtypeStruct((n_rows, dg * lanes), BF16),
        compiler_params=_cparams(("arbitrary",)),
        name="moe_gather",
    )(src, tok3)


def _grouped_rows_kernel(we_ref, wrow_ref, wnp_ref, tail_ref, x_hbm, *refs, n_w, tn, tile_fn):
    w_refs = refs[:2 * n_w]
    out_hbm = refs[2 * n_w]
    xbuf = refs[2 * n_w + 1]
    obufs = refs[2 * n_w + 2:2 * n_w + 2 + len(MOE_TILE_PIECES)]
    zbuf, xsem, osem, zsem, cnt_ref = refs[2 * n_w + 2 + len(MOE_TILE_PIECES):]
    wi = pl.program_id(0)
    j = pl.program_id(1)
    n_pieces = wnp_ref[wi]
    row0 = wrow_ref[wi]
    is_first = jnp.logical_and(wi == 0, j == 0)
    is_last = jnp.logical_and(wi == pl.num_programs(0) - 1, j == pl.num_programs(1) - 1)

    @pl.when(is_first)
    def _():
        for ci in range(len(MOE_TILE_PIECES)):
            cnt_ref[ci] = 0

    def piece_copy(p):
        return pltpu.make_async_copy(x_hbm.at[pl.ds(pl.multiple_of(row0 + p * MOE_PIECE, MOE_PIECE), MOE_PIECE)],
                                     xbuf.at[pl.ds(pl.multiple_of(p * MOE_PIECE, MOE_PIECE), MOE_PIECE)],
                                     xsem.at[0])

    @pl.when(j == 0)
    def _():
        def start(p, c):
            piece_copy(p).start()
            return c

        def wait(p, c):
            piece_copy(p).wait()
            return c
        lax.fori_loop(0, n_pieces, start, 0)
        lax.fori_loop(0, n_pieces, wait, 0)

    col0 = pl.multiple_of(j * tn, tn)

    def out_copy(ci, slot, r0, rows):
        dst = out_hbm.at[pl.ds(pl.multiple_of(row0 + r0, MOE_PIECE), rows), pl.ds(col0, tn)]
        return pltpu.make_async_copy(obufs[ci].at[slot], dst, osem.at[ci, slot])

    def emit(ci, r0):
        rows = MOE_TILE_PIECES[ci] * MOE_PIECE
        r0 = pl.multiple_of(r0, MOE_PIECE)
        val = tile_fn(xbuf[pl.ds(r0, rows), :], w_refs)
        c = cnt_ref[ci]
        slot = c % 2

        @pl.when(c >= 2)
        def _():
            out_copy(ci, slot, r0, rows).wait()

        obufs[ci][slot] = val.astype(obufs[ci].dtype)
        out_copy(ci, slot, r0, rows).start()
        cnt_ref[ci] = c + 1

    remaining = n_pieces
    base = 0
    for ci, tp in enumerate(MOE_TILE_PIECES):
        n_t = remaining // tp

        def body(t, c, ci=ci, base=base, tp=tp):
            emit(ci, base + t * (tp * MOE_PIECE))
            return c
        lax.fori_loop(0, n_t, body, 0)
        base = base + n_t * (tp * MOE_PIECE)
        remaining = remaining - n_t * tp

    @pl.when(is_last)
    def _():
        for ci, tp in enumerate(MOE_TILE_PIECES):
            c = cnt_ref[ci]
            for back in (1, 2):
                @pl.when(c >= back)
                def _(ci=ci, back=back, c=c, tp=tp):
                    out_copy(ci, (c - back) % 2, 0, tp * MOE_PIECE).wait()
        zbuf[...] = jnp.zeros(zbuf.shape, zbuf.dtype)
        first_tail = tail_ref[0]
        n_total = out_hbm.shape[0] // MOE_PIECE

        def zcopy(p):
            return pltpu.make_async_copy(zbuf, out_hbm.at[pl.ds(pl.multiple_of(p * MOE_PIECE, MOE_PIECE), MOE_PIECE)],
                                         zsem.at[0])

        def zstart(p, c):
            zcopy(p).start()
            return c

        def zwait(p, c):
            zcopy(p).wait()
            return c
        lax.fori_loop(first_tail, n_total, zstart, 0)
        lax.fori_loop(first_tail, n_total, zwait, 0)


def _up_tile(x, w_refs):
    wg_ref, wu_ref, bg_ref, bu_ref = w_refs
    g = jnp.dot(x, wg_ref[0].astype(BF16), preferred_element_type=F32) + bg_ref[0]
    up = jnp.dot(x, wu_ref[0].astype(BF16), preferred_element_type=F32) + bu_ref[0]
    g = jnp.minimum(g, SWIGLU_LIMIT)
    up = jnp.clip(up, -SWIGLU_LIMIT, SWIGLU_LIMIT)
    return (up + 1.0) * (g * _sigmoid(SWIGLU_ALPHA * g))


def _down_tile(x, w_refs):
    wd_ref, bd_ref = w_refs
    return jnp.dot(x, wd_ref[0].astype(BF16), preferred_element_type=F32) + bd_ref[0]


def _grouped_rows(tables, xs, w_specs, w_args, n_out, tn, out_dtype, tile_fn, name):
    we, wrow, wnp, tail = tables
    n_rows, k = xs.shape
    n_w = len(w_args) // 2
    scratch = [pltpu.VMEM((MOE_CAP_PIECES * MOE_PIECE, k), BF16)]
    scratch += [pltpu.VMEM((2, tp * MOE_PIECE, tn), out_dtype) for tp in MOE_TILE_PIECES]
    scratch += [pltpu.VMEM((MOE_PIECE, n_out), out_dtype),
                pltpu.SemaphoreType.DMA((1,)),
                pltpu.SemaphoreType.DMA((len(MOE_TILE_PIECES), 2)),
                pltpu.SemaphoreType.DMA((1,)),
                pltpu.SMEM((len(MOE_TILE_PIECES),), I32)]
    return pl.pallas_call(
        functools.partial(_grouped_rows_kernel, n_w=n_w, tn=tn, tile_fn=tile_fn),
        grid_spec=pltpu.PrefetchScalarGridSpec(
            num_scalar_prefetch=4,
            grid=(we.shape[0], n_out // tn),
            in_specs=[pl.BlockSpec(memory_space=pl.ANY)] + w_specs,
            out_specs=pl.BlockSpec(memory_space=pl.ANY),
            scratch_shapes=scratch),
        out_shape=jax.ShapeDtypeStruct((n_rows, n_out), out_dtype),
        compiler_params=_cparams(("arbitrary", "arbitrary")),
        name=name,
    )(we, wrow, wnp, tail, xs, *w_args)


def _moe_up(tables, xs, w_gate_up, b_gate_up, tf=256):
    n_exp, d, two_f = w_gate_up.shape
    f = two_f // 2
    nj = f // tf
    b3 = b_gate_up.reshape(n_exp, 1, two_f)
    w_specs = [pl.BlockSpec((1, d, tf), lambda wi, j, we, *_: (we[wi], 0, j)),
               pl.BlockSpec((1, d, tf), lambda wi, j, we, *_: (we[wi], 0, nj + j)),
               pl.BlockSpec((1, 1, tf), lambda wi, j, we, *_: (we[wi], 0, j)),
               pl.BlockSpec((1, 1, tf), lambda wi, j, we, *_: (we[wi], 0, nj + j))]
    return _grouped_rows(tables, xs, w_specs, [w_gate_up, w_gate_up, b3, b3], f, tf, BF16, _up_tile, "moe_up")


def _moe_down(tables, act, w_down, b_down, tn=512):
    n_exp, f, d = w_down.shape
    w_specs = [pl.BlockSpec((1, f, tn), lambda wi, j, we, *_: (we[wi], 0, j)),
               pl.BlockSpec((1, 1, tn), lambda wi, j, we, *_: (we[wi], 0, j))]
    return _grouped_rows(tables, act, w_specs, [w_down, b_down.reshape(n_exp, 1, d)], d, tn, F32, _down_tile,
                         "moe_down")


def _combine_kernel(dest_ref, o_hbm, gate_ref, hp_ref, hs_ref, g_ref, yp_ref, ys_ref, buf, sem, *, tm,
                    n_prompt_tiles):
    i = pl.program_id(0)
    n = pl.num_programs(0)

    def issue(tile, slot):
        def one(r, carry):
            for kk in range(TOP_K):
                p = dest_ref[(tile * tm + r) * TOP_K + kk]
                pltpu.make_async_copy(o_hbm.at[pl.ds(p, 1), :], buf.at[slot, kk, pl.ds(r, 1), :],
                                      sem.at[slot]).start()
            return carry
        lax.fori_loop(0, tm, one, 0, unroll=2)

    @pl.when(i == 0)
    def _():
        issue(0, 0)

    @pl.when(i + 1 < n)
    def _():
        issue(i + 1, (i + 1) % 2)

    slot = i % 2
    for kk in range(TOP_K):
        pltpu.make_async_copy(o_hbm.at[pl.ds(0, tm), :], buf.at[slot, kk], sem.at[slot]).wait()

    gates = gate_ref[...]

    def finish(h_ref, y_ref):
        acc = h_ref[...]
        for kk in range(TOP_K):
            acc = acc + gates[:, kk:kk + 1] * buf[slot, kk]
        y_ref[...] = _rms(acc, g_ref[...])

    @pl.when(i < n_prompt_tiles)
    def _():
        finish(hp_ref, yp_ref)

    @pl.when(i >= n_prompt_tiles)
    def _():
        finish(hs_ref, ys_ref)


def _combine(dest_flat, o_sorted, gates, hp, hs, g_final):
    tp, d = hp.shape
    ts = hs.shape[0]
    tm = min(128, ts)
    assert tp % tm == 0 and ts % tm == 0
    npt, nst = tp // tm, ts // tm
    p_blk = lambda: pl.BlockSpec((tm, d), lambda i, dref: (jnp.minimum(i, npt - 1), 0))
    s_blk = lambda: pl.BlockSpec((tm, d), lambda i, dref: (jnp.maximum(i - npt, 0), 0))
    return pl.pallas_call(
        functools.partial(_combine_kernel, tm=tm, n_prompt_tiles=npt),
        grid_spec=pltpu.PrefetchScalarGridSpec(
            num_scalar_prefetch=1,
            grid=(npt + nst,),
            in_specs=[pl.BlockSpec(memory_space=pl.ANY),
                      pl.BlockSpec((tm, LANES), lambda i, dref: (i, 0)),
                      p_blk(), s_blk(),
                      pl.BlockSpec((1, d), lambda i, dref: (0, 0))],
            out_specs=[p_blk(), s_blk()],
            scratch_shapes=[pltpu.VMEM((2, TOP_K, tm, d), F32), pltpu.SemaphoreType.DMA((2,))]),
        out_shape=[jax.ShapeDtypeStruct((tp, d), F32), jax.ShapeDtypeStruct((ts, d), F32)],
        compiler_params=_cparams(("arbitrary",)),
        name="moe_combine",
    )(dest_flat, o_sorted, gates, hp, hs, g_final.reshape(1, d))


def _mixer_front(x2, g_mix, w_in_t, b_i, b_f, n_heads, n_main):
    d = x2.shape[1]
    wgt = jnp.pad(w_in_t[n_main:], ((0, LANES - 2 * n_heads), (0, 0)))
    b_gate = jnp.pad(jnp.concatenate([b_i, b_f]), (0, LANES - 2 * n_heads)).reshape(1, LANES)
    xn, gates = _norm_gates(x2, g_mix.reshape(1, d), wgt, b_gate)
    u = _proj([xn], w_in_t, n_main, w_is_nk=True)
    return u, gates


def _moe(hp, hs, g_ffn, w_router_t, b_router, w_gate_up, b_gate_up, w_down, b_down, g_final):
    tp, d = hp.shape
    ts = hs.shape[0]
    t = tp + ts
    n_exp = w_router_t.shape[0]
    wrt = jnp.pad(w_router_t, ((0, LANES - n_exp), (0, 0)))
    br = jnp.pad(b_router, (0, LANES - n_exp), constant_values=NEG_BIG).reshape(1, LANES)
    tok3, idx_o, gate_o, rank_o, cnt_o = _router(hp, hs, g_ffn.reshape(1, d), wrt, br, tm=min(256, ts))

    n_rows = _round_up(t * TOP_K + n_exp * (MOE_PIECE - 1), GATHER_TILE)
    n_work = n_exp + (n_rows // MOE_PIECE) // MOE_CAP_PIECES
    counts = cnt_o[0, :n_exp]
    pieces = (counts + MOE_PIECE - 1) // MOE_PIECE
    first_piece = jnp.cumsum(pieces) - pieces
    chunks = (pieces + MOE_CAP_PIECES - 1) // MOE_CAP_PIECES
    cum_chunks = jnp.cumsum(chunks)
    wid = jnp.arange(n_work, dtype=I32)
    used = wid < cum_chunks[-1]
    e_of_w = jnp.sum((cum_chunks[None, :] <= wid[:, None]).astype(I32), axis=1)
    last_e = jnp.max(jnp.where(chunks > 0, jnp.arange(n_exp, dtype=I32), 0))
    we = jnp.where(used, jnp.minimum(e_of_w, n_exp - 1), last_e).astype(I32)
    chunk_in_e = wid - (cum_chunks - chunks)[we]
    wrow = jnp.where(used, (first_piece[we] + chunk_in_e * MOE_CAP_PIECES) * MOE_PIECE, 0).astype(I32)
    wnp = jnp.where(used, jnp.clip(pieces[we] - chunk_in_e * MOE_CAP_PIECES, 0, MOE_CAP_PIECES), 0).astype(I32)
    tail = jnp.sum(pieces).reshape(1).astype(I32)
    tables = (we, wrow, wnp, tail)

    dest = (first_piece[idx_o[:, :TOP_K]] * MOE_PIECE + rank_o[:, :TOP_K]).reshape(-1)
    token_of_pair = jnp.repeat(jnp.arange(t, dtype=I32), TOP_K)
    src = jnp.zeros((n_rows,), I32).at[dest].set(token_of_pair)

    xs = _gather_rows(src, tok3, n_rows, GATHER_TILE)
    act = _moe_up(tables, xs, w_gate_up, b_gate_up)
    o_sorted = _moe_down(tables, act, w_down, b_down)
    return _combine(dest, o_sorted, gate_o, hp, hs, g_final)


def kernel(x_prompt, x_sample, state_conv, state_C, state_n, state_m, g_mix, w_in, b_i, b_f, w_dw, b_dw,
           conv_ln_g, conv_ln_b, mlstm_norm_g, w_out, g_ffn, w_router, b_router, w_gate_up, b_gate_up,
           w_down, b_down, g_final):
    b, s, d = x_prompt.shape
    bd, sd, _ = x_sample.shape
    depth = w_in.shape[0]
    assert depth == 1, "the final norm is fused into the MoE combine of the single layer"
    n_heads, dk, dv = state_C.shape[2], state_C.shape[3], state_C.shape[4]
    c_conv = state_conv.shape[3]
    n_main = 2 * c_conv + 2 * n_heads * dk + 2 * n_heads * dv
    l = 0
    w_in_t = jnp.swapaxes(w_in[l], 0, 1)
    w_router_t = jnp.swapaxes(w_router[l], 0, 1)

    xp2 = x_prompt.reshape(b * s, d)
    up, gates_p = _mixer_front(xp2, g_mix[l], w_in_t, b_i[l], b_f[l], n_heads, n_main)
    up3 = up.reshape(b, s, n_main)
    conv_p, cbp = _conv_seq(up3, w_dw[l], b_dw[l], conv_ln_g[l], conv_ln_b[l], c_conv)
    hm_p, c_p, n_p, m_p = _mlstm(up3, gates_p.reshape(b, s, LANES), mlstm_norm_g[l], None, n_heads, dk, dv, c_conv)
    hp = _proj([conv_p.reshape(b * s, c_conv), hm_p.reshape(b * s, n_heads * dv)], w_out[l], d, res=xp2)

    xs2 = x_sample.reshape(bd * sd, d)
    us, gates_s = _mixer_front(xs2, g_mix[l], w_in_t, b_i[l], b_f[l], n_heads, n_main)
    us3 = us.reshape(bd, sd, n_main)
    ug = jnp.transpose(us3[:, :, :2 * c_conv], (1, 0, 2))
    state_t = jnp.transpose(state_conv[l], (1, 0, 2))
    conv_s_t, cbs_t = _conv_step(ug, state_t, w_dw[l], b_dw[l], conv_ln_g[l], conv_ln_b[l], c_conv)
    conv_s = jnp.transpose(conv_s_t, (1, 0, 2)).reshape(bd * sd, c_conv)
    cbs = jnp.transpose(cbs_t, (1, 0, 2))
    hm_s, c_s, n_s, m_s = _mlstm(us3, gates_s.reshape(bd, sd, LANES), mlstm_norm_g[l],
                                 (state_C[l], state_n[l], state_m[l]), n_heads, dk, dv, c_conv)
    hs = _proj([conv_s, hm_s.reshape(bd * sd, n_heads * dv)], w_out[l], d, res=xs2)

    yp, ys = _moe(hp, hs, g_ffn[l], w_router_t, b_router[l], w_gate_up[l], b_gate_up[l], w_down[l], b_down[l],
                  g_final)
    return (yp.reshape(b, s, d), ys.reshape(bd, sd, d), cbp[None], c_p[None], n_p[None], m_p[None],
            cbs[None], c_s[None], n_s[None], m_s[None])
```

```python
import functools

import jax
import jax.numpy as jnp
from jax import lax
from jax.experimental import pallas as pl
from jax.experimental.pallas import tpu as pltpu

F32 = jnp.float32
BF16 = jnp.bfloat16
I32 = jnp.int32

NORM_EPS = 1e-5
TOP_K = 4
SWIGLU_LIMIT = 7.0
SWIGLU_ALPHA = 1.702
LANES = 128
SUBLANES = 8
VMEM_LIMIT_BYTES = 56 * 1024 * 1024
NEG_BIG = -1e30

MOE_PIECE = 128
MOE_CAP_PIECES = 16
MOE_TILE_PIECES = (8, 2, 1)
W_RING = 3
GATHER_TILE = 256


def _round_up(a, b):
    return (a + b - 1) // b * b


def _cparams(sem):
    return pltpu.CompilerParams(dimension_semantics=sem, vmem_limit_bytes=VMEM_LIMIT_BYTES)


def _sigmoid(x):
    return 1.0 / (1.0 + jnp.exp(-x))


def _log_sigmoid(x):
    return jnp.minimum(x, 0.0) - jnp.log(1.0 + jnp.exp(-jnp.abs(x)))


def _rms(x, g):
    return x * lax.rsqrt(jnp.mean(x * x, axis=-1, keepdims=True) + NORM_EPS) * g


_NT = (((1,), (1,)), ((), ()))


def _norm_gates_kernel(x_ref, g_ref, wgt_ref, bg_ref, xn_ref, gt_ref):
    yb = _rms(x_ref[...], g_ref[...]).astype(BF16)
    xn_ref[...] = yb
    gt_ref[...] = lax.dot_general(yb, wgt_ref[...].astype(BF16), _NT, preferred_element_type=F32) + bg_ref[...]


def _norm_gates(x, g, wgt, bg):
    tn, d = x.shape
    tm = min(512, tn)
    return pl.pallas_call(
        _norm_gates_kernel,
        grid=(tn // tm,),
        in_specs=[pl.BlockSpec((tm, d), lambda i: (i, 0)),
                  pl.BlockSpec((1, d), lambda i: (0, 0)),
                  pl.BlockSpec((LANES, d), lambda i: (0, 0)),
                  pl.BlockSpec((1, LANES), lambda i: (0, 0))],
        out_specs=[pl.BlockSpec((tm, d), lambda i: (i, 0)),
                   pl.BlockSpec((tm, LANES), lambda i: (i, 0))],
        out_shape=[jax.ShapeDtypeStruct((tn, d), BF16), jax.ShapeDtypeStruct((tn, LANES), F32)],
        compiler_params=_cparams(("arbitrary",)),
        name="norm_gates",
    )(x, g, wgt, bg)


def _proj_kernel(*refs, n_lhs, has_res, w_is_nk):
    lhs_refs = refs[:n_lhs]
    w_ref = refs[n_lhs]
    res_ref = refs[n_lhs + 1] if has_res else None
    o_ref = refs[-1]
    acc = None
    k0 = 0
    for r in lhs_refs:
        kk = r.shape[1]
        lhs = r[...].astype(BF16)
        if w_is_nk:
            part = lax.dot_general(lhs, w_ref[:, k0:k0 + kk].astype(BF16), _NT, preferred_element_type=F32)
        else:
            part = jnp.dot(lhs, w_ref[k0:k0 + kk, :].astype(BF16), preferred_element_type=F32)
        acc = part if acc is None else acc + part
        k0 += kk
    if has_res:
        acc = acc + res_ref[...]
    o_ref[...] = acc


def _proj(lhs_list, w, n_out, res=None, w_is_nk=False, tn=512):
    rows = lhs_list[0].shape[0]
    k_total = w.shape[1] if w_is_nk else w.shape[0]
    assert sum(a.shape[1] for a in lhs_list) == k_total and n_out % tn == 0
    tm = min(1024, rows)
    in_specs = [pl.BlockSpec((tm, a.shape[1]), lambda j, i: (i, 0)) for a in lhs_list]
    if w_is_nk:
        in_specs.append(pl.BlockSpec((tn, k_total), lambda j, i: (j, 0)))
    else:
        in_specs.append(pl.BlockSpec((k_total, tn), lambda j, i: (0, j)))
    args = list(lhs_list) + [w]
    if res is not None:
        in_specs.append(pl.BlockSpec((tm, tn), lambda j, i: (i, j)))
        args.append(res)
    return pl.pallas_call(
        functools.partial(_proj_kernel, n_lhs=len(lhs_list), has_res=res is not None, w_is_nk=w_is_nk),
        grid=(n_out // tn, rows // tm),
        in_specs=in_specs,
        out_specs=pl.BlockSpec((tm, tn), lambda j, i: (i, j)),
        out_shape=jax.ShapeDtypeStruct((rows, n_out), F32),
        compiler_params=_cparams(("arbitrary", "arbitrary")),
        name="proj",
    )(*args)


def _layernorm_swish(y, g, b):
    mu = jnp.mean(y, axis=-1, keepdims=True)
    yc = y - mu
    var = jnp.mean(yc * yc, axis=-1, keepdims=True)
    z = yc * lax.rsqrt(var + NORM_EPS) * g + b
    return z * _sigmoid(z)


def _conv_seq_kernel(ua_ref, ub_ref, wdw_ref, bdw_ref, lng_ref, lnb_ref, out_ref, nbuf_ref, full_ref, y_ref,
                     *, tl, kw, cc):
    halo = kw - 1
    padh = _round_up(halo, SUBLANES)
    off = padh - halo
    c = ua_ref.shape[-1]
    l = pl.program_id(1)

    @pl.when(l == 0)
    def _():
        full_ref[off:padh, :] = jnp.zeros((halo, c), F32)

    full_ref[padh:padh + tl, :] = ua_ref[0] * _sigmoid(ub_ref[0])

    first = {}
    last = {}
    for w in range(kw):
        first.setdefault((off + w) % SUBLANES, off + w)
        last[(off + w) % SUBLANES] = off + w
    for c0 in range(0, c, cc):
        shifted = {r: full_ref[first[r]:last[r] + tl, c0:c0 + cc] for r in first}
        acc = jnp.zeros((tl, cc), F32) + bdw_ref[:, c0:c0 + cc]
        for w in range(kw):
            r = (off + w) % SUBLANES
            a = off + w - first[r]
            acc = acc + shifted[r][a:a + tl] * wdw_ref[w:w + 1, c0:c0 + cc]
        y_ref[:, c0:c0 + cc] = acc

    out_ref[0] = _layernorm_swish(y_ref[...], lng_ref[...], lnb_ref[...]).astype(BF16)

    tail = full_ref[tl + off:tl + padh, :]

    @pl.when(l == pl.num_programs(1) - 1)
    def _():
        nbuf_ref[0] = tail

    full_ref[off:padh, :] = tail


def _conv_seq(u3, w_dw, b_dw, ln_g, ln_b, c):
    b, l, _ = u3.shape
    kw = w_dw.shape[0]
    halo = kw - 1
    tl = 64
    assert l % tl == 0
    padh = _round_up(halo, SUBLANES)
    return pl.pallas_call(
        functools.partial(_conv_seq_kernel, tl=tl, kw=kw, cc=min(256, c)),
        grid=(b, l // tl),
        in_specs=[pl.BlockSpec((1, tl, c), lambda bi, li: (bi, li, 0)),
                  pl.BlockSpec((1, tl, c), lambda bi, li: (bi, li, 1)),
                  pl.BlockSpec((kw, c), lambda bi, li: (0, 0))] + [pl.BlockSpec((1, c), lambda bi, li: (0, 0))] * 3,
        out_specs=[pl.BlockSpec((1, tl, c), lambda bi, li: (bi, li, 0)),
                   pl.BlockSpec((1, halo, c), lambda bi, li: (bi, 0, 0))],
        out_shape=[jax.ShapeDtypeStruct((b, l, c), BF16), jax.ShapeDtypeStruct((b, halo, c), F32)],
        scratch_shapes=[pltpu.VMEM((padh + tl, c), F32), pltpu.VMEM((tl, c), F32)],
        compiler_params=_cparams(("arbitrary", "arbitrary")),
        name="conv_seq",
    )(u3, u3, w_dw, b_dw.reshape(1, c), ln_g.reshape(1, c), ln_b.reshape(1, c))


def _conv_step_kernel(ua_ref, ub_ref, st_ref, wdw_ref, bdw_ref, lng_ref, lnb_ref, out_ref, nst_ref, y_ref,
                      *, ls, kw, cc):
    halo = kw - 1
    j = pl.program_id(0)
    glu = [ua_ref[t] * _sigmoid(ub_ref[t]) for t in range(ls)]

    def window(idx):
        return st_ref[idx] if idx < halo else glu[idx - halo]

    for t in range(ls):
        acc = jnp.zeros(glu[0].shape, F32) + bdw_ref[...]
        for w in range(kw):
            acc = acc + window(t + w) * wdw_ref[w:w + 1, :]
        y_ref[j, t] = acc
    for idx in range(halo):
        nst_ref[idx] = window(ls + idx)

    @pl.when(j == pl.num_programs(0) - 1)
    def _():
        for t in range(ls):
            y = jnp.concatenate([y_ref[jj, t] for jj in range(y_ref.shape[0])], axis=-1)
            out_ref[t] = _layernorm_swish(y, lng_ref[...], lnb_ref[...]).astype(BF16)


def _conv_step(ug, state_t, w_dw, b_dw, ln_g, ln_b, c):
    ls, b, _ = ug.shape
    kw = w_dw.shape[0]
    halo = kw - 1
    cc = min(256, c)
    nj = c // cc
    return pl.pallas_call(
        functools.partial(_conv_step_kernel, ls=ls, kw=kw, cc=cc),
        grid=(nj,),
        in_specs=[pl.BlockSpec((ls, b, cc), lambda j: (0, 0, j)),
                  pl.BlockSpec((ls, b, cc), lambda j: (0, 0, nj + j)),
                  pl.BlockSpec((halo, b, cc), lambda j: (0, 0, j)),
                  pl.BlockSpec((kw, cc), lambda j: (0, j)),
                  pl.BlockSpec((1, cc), lambda j: (0, j)),
                  pl.BlockSpec((1, c), lambda j: (0, 0)),
                  pl.BlockSpec((1, c), lambda j: (0, 0))],
        out_specs=[pl.BlockSpec((ls, b, c), lambda j: (0, 0, 0)),
                   pl.BlockSpec((halo, b, cc), lambda j: (0, 0, j))],
        out_shape=[jax.ShapeDtypeStruct((ls, b, c), BF16), jax.ShapeDtypeStruct((halo, b, c), F32)],
        scratch_shapes=[pltpu.VMEM((nj, ls, b, cc), F32)],
        compiler_params=_cparams(("arbitrary",)),
        name="conv_step",
    )(ug, ug, state_t, w_dw, b_dw.reshape(1, c), ln_g.reshape(1, c), ln_b.reshape(1, c))


def _mlstm_kernel(*refs, lc, dk, dv, n_heads, has_state):
    if has_state:
        (q_ref, k_ref, v_ref, o_ref, gc_ref, gr_ref, ng_ref, c0_ref, n0_ref, m0_ref,
         hm_ref, co_ref, no_ref, mo_ref, c_s, n_s, m_s) = refs
    else:
        (q_ref, k_ref, v_ref, o_ref, gc_ref, gr_ref, ng_ref,
         hm_ref, co_ref, no_ref, mo_ref, c_s, n_s, m_s) = refs
    j = pl.program_id(1)

    @pl.when(j == 0)
    def _():
        if has_state:
            c_s[...] = c0_ref[0]
            n_s[...] = n0_ref[0]
            m_s[...] = m0_ref[0]
        else:
            c_s[...] = jnp.zeros(c_s.shape, F32)
            n_s[...] = jnp.zeros(n_s.shape, F32)
            m_s[...] = jnp.zeros(m_s.shape, F32)

    gc = gc_ref[0]
    gr = gr_ref[0]
    ri = lax.broadcasted_iota(I32, (lc, lc), 0)
    ci = lax.broadcasted_iota(I32, (lc, lc), 1)
    causal = ri >= ci

    for h in range(n_heads):
        q = q_ref[0, :, h * dk:(h + 1) * dk] * (dk ** -0.5)
        kf = k_ref[0, :, h * dk:(h + 1) * dk]
        qb = q.astype(BF16)
        kb = kf.astype(BF16)
        vb = v_ref[0, :, h * dv:(h + 1) * dv].astype(BF16)

        gi_col = gc[:, h:h + 1]
        gf_col = gc[:, n_heads + h:n_heads + h + 1]
        gi_row = gr[h:h + 1, :]
        gf_row = gr[n_heads + h:n_heads + h + 1, :]
        lf_col = _log_sigmoid(gf_col)
        lf_row = _log_sigmoid(gf_row)
        a_col = jnp.sum(jnp.where(causal, lf_row, 0.0), axis=1, keepdims=True)
        a_row = jnp.sum(jnp.where(ri <= ci, lf_col, 0.0), axis=0, keepdims=True)

        m_prev = m_s[h]
        d_log = jnp.where(causal, a_col - a_row + gi_row, -jnp.inf)
        inter = a_col + m_prev
        m_row = jnp.maximum(inter, jnp.max(d_log, axis=1, keepdims=True))
        w_intra = jnp.exp(d_log - m_row)
        w_inter = jnp.exp(inter - m_row)

        s = lax.dot_general(qb, kb, _NT, preferred_element_type=F32) * w_intra
        c_prev = c_s[h]
        n_prev = n_s[h]
        q_c = jnp.dot(qb, c_prev.astype(BF16), preferred_element_type=F32)
        num = jnp.dot(s.astype(BF16), vb, preferred_element_type=F32) + w_inter * q_c
        q_n = jnp.sum(q * n_prev, axis=1, keepdims=True)
        den_raw = jnp.sum(s, axis=1, keepdims=True) + w_inter * q_n
        den = jnp.maximum(jnp.abs(den_raw), jnp.exp(-m_row))
        hh = num / den

        hn = _rms(hh, ng_ref[:, h * dv:(h + 1) * dv])
        hm_ref[0, :, h * dv:(h + 1) * dv] = (hn * _sigmoid(o_ref[0, :, h * dv:(h + 1) * dv])).astype(BF16)

        a_end = a_col[lc - 1:lc, :]
        m_new = m_row[lc - 1:lc, :]
        decay = jnp.exp(a_end + m_prev - m_new)
        w_end = jnp.exp(a_end - a_col + gi_col - m_new)
        kw = kf * w_end
        c_new = decay * c_prev + lax.dot_general(kw.astype(BF16), vb, (((0,), (0,)), ((), ())),
                                                 preferred_element_type=F32)
        n_new = decay * n_prev + jnp.sum(kw, axis=0, keepdims=True)
        c_s[h] = c_new
        n_s[h] = n_new
        m_s[h] = m_new

    @pl.when(j == pl.num_programs(1) - 1)
    def _():
        co_ref[0] = c_s[...]
        no_ref[0] = n_s[...]
        mo_ref[0] = m_s[...]


def _mlstm(u3, gates3, norm_g, state, n_heads, dk, dv, c_conv):
    b, l, _ = u3.shape
    lc = 256 if l % 256 == 0 else l
    nc = l // lc
    wq, wv = n_heads * dk, n_heads * dv
    assert (2 * c_conv) % wq == 0 and (2 * c_conv + 2 * wq) % wv == 0
    q0 = 2 * c_conv // wq
    v0 = (2 * c_conv + 2 * wq) // wv
    n_gate_rows = _round_up(2 * n_heads, SUBLANES)
    gates_row = jnp.transpose(gates3[:, :, :n_gate_rows], (0, 2, 1))
    has_state = state is not None
    in_specs = [pl.BlockSpec((1, lc, wq), lambda bi, ji: (bi, ji, q0)),
                pl.BlockSpec((1, lc, wq), lambda bi, ji: (bi, ji, q0 + 1)),
                pl.BlockSpec((1, lc, wv), lambda bi, ji: (bi, ji, v0)),
                pl.BlockSpec((1, lc, wv), lambda bi, ji: (bi, ji, v0 + 1)),
                pl.BlockSpec((1, lc, LANES), lambda bi, ji: (bi, ji, 0)),
                pl.BlockSpec((1, n_gate_rows, lc), lambda bi, ji: (bi, 0, ji)),
                pl.BlockSpec((1, wv), lambda bi, ji: (0, 0))]
    args = [u3, u3, u3, u3, gates3, gates_row, norm_g.reshape(1, wv)]
    state_specs = lambda: [pl.BlockSpec((1, n_heads, dk, dv), lambda bi, ji: (bi, 0, 0, 0)),
                           pl.BlockSpec((1, n_heads, 1, dk), lambda bi, ji: (bi, 0, 0, 0)),
                           pl.BlockSpec((1, n_heads, 1, 1), lambda bi, ji: (bi, 0, 0, 0))]
    if has_state:
        c0, n0, m0 = state
        in_specs += state_specs()
        args += [c0, n0.reshape(b, n_heads, 1, dk), m0.reshape(b, n_heads, 1, 1)]
    hm, c_f, n_f, m_f = pl.pallas_call(
        functools.partial(_mlstm_kernel, lc=lc, dk=dk, dv=dv, n_heads=n_heads, has_state=has_state),
        grid=(b, nc),
        in_specs=in_specs,
        out_specs=[pl.BlockSpec((1, lc, wv), lambda bi, ji: (bi, ji, 0))] + state_specs(),
        out_shape=[jax.ShapeDtypeStruct((b, l, wv), BF16),
                   jax.ShapeDtypeStruct((b, n_heads, dk, dv), F32),
                   jax.ShapeDtypeStruct((b, n_heads, 1, dk), F32),
                   jax.ShapeDtypeStruct((b, n_heads, 1, 1), F32)],
        scratch_shapes=[pltpu.VMEM((n_heads, dk, dv), F32), pltpu.VMEM((n_heads, 1, dk), F32),
                        pltpu.VMEM((n_heads, 1, 1), F32)],
        compiler_params=_cparams(("arbitrary", "arbitrary")),
        name="mlstm",
    )(*args)
    return hm, c_f, n_f.reshape(b, n_heads, dk), m_f.reshape(b, n_heads)


def _router_kernel(hp_ref, hs_ref, g_ref, wrt_ref, br_ref, tok_ref, idx_ref, gate_ref, rank_ref, cnt_ref,
                   carry_ref, *, n_prompt_tiles):
    i = pl.program_id(0)

    @pl.when(i == 0)
    def _():
        carry_ref[...] = jnp.zeros(carry_ref.shape, F32)

    x = jnp.where(i < n_prompt_tiles, hp_ref[...], hs_ref[...])
    tok = _rms(x, g_ref[...])
    tok_ref[...] = tok.reshape(tok_ref.shape)
    lg = lax.dot_general(tok, wrt_ref[...], _NT, preferred_element_type=F32,
                         precision=lax.Precision.HIGHEST) + br_ref[...]
    tm = lg.shape[0]
    lane = lax.broadcasted_iota(I32, lg.shape, 1).astype(F32)
    vals, idxs = [], []
    for _ in range(TOP_K):
        mx = jnp.max(lg, axis=1, keepdims=True)
        ik = jnp.min(jnp.where(lg == mx, lane, float(LANES)), axis=1, keepdims=True)
        vals.append(mx)
        idxs.append(ik)
        lg = jnp.where(lane == ik, -jnp.inf, lg)
    exps = [jnp.exp(v - vals[0]) for v in vals]
    ssum = exps[0]
    for e in exps[1:]:
        ssum = ssum + e
    gates = [e / ssum for e in exps]

    multi = jnp.zeros(lg.shape, F32)
    for ik in idxs:
        multi = multi + (lane == ik).astype(F32)
    ri = lax.broadcasted_iota(I32, (tm, tm), 0)
    ci = lax.broadcasted_iota(I32, (tm, tm), 1)
    below = (ri > ci).astype(BF16)
    cum = jnp.dot(below, multi.astype(BF16), preferred_element_type=F32) + carry_ref[...]
    ranks = [jnp.sum(jnp.where(lane == ik, cum, 0.0), axis=1, keepdims=True) for ik in idxs]
    carry = carry_ref[...] + jnp.sum(multi, axis=0, keepdims=True)
    carry_ref[...] = carry
    cnt_ref[...] = carry.astype(I32)

    idx_o = jnp.zeros(lg.shape, F32)
    gate_o = jnp.zeros(lg.shape, F32)
    rank_o = jnp.zeros(lg.shape, F32)
    for kk in range(TOP_K):
        sel = lane == float(kk)
        idx_o = jnp.where(sel, idxs[kk], idx_o)
        gate_o = jnp.where(sel, gates[kk], gate_o)
        rank_o = jnp.where(sel, ranks[kk], rank_o)
    idx_ref[...] = idx_o.astype(I32)
    gate_ref[...] = gate_o
    rank_ref[...] = rank_o.astype(I32)


def _router(hp, hs, g, wrt, br, tm):
    tp, d = hp.shape
    ts = hs.shape[0]
    assert tp % tm == 0 and ts % tm == 0
    npt, nst = tp // tm, ts // tm
    t = tp + ts
    dg = d // LANES
    blk = lambda w: pl.BlockSpec((tm, w), lambda i: (i, 0))
    return pl.pallas_call(
        functools.partial(_router_kernel, n_prompt_tiles=npt),
        grid=(npt + nst,),
        in_specs=[pl.BlockSpec((tm, d), lambda i: (jnp.minimum(i, npt - 1), 0)),
                  pl.BlockSpec((tm, d), lambda i: (jnp.maximum(i - npt, 0), 0)),
                  pl.BlockSpec((1, d), lambda i: (0, 0)),
                  pl.BlockSpec((LANES, d), lambda i: (0, 0)),
                  pl.BlockSpec((1, LANES), lambda i: (0, 0))],
        out_specs=[pl.BlockSpec((tm, dg, LANES), lambda i: (i, 0, 0)), blk(LANES), blk(LANES), blk(LANES),
                   pl.BlockSpec((1, LANES), lambda i: (0, 0))],
        out_shape=[jax.ShapeDtypeStruct((t, dg, LANES), F32), jax.ShapeDtypeStruct((t, LANES), I32),
                   jax.ShapeDtypeStruct((t, LANES), F32), jax.ShapeDtypeStruct((t, LANES), I32),
                   jax.ShapeDtypeStruct((1, LANES), I32)],
        scratch_shapes=[pltpu.VMEM((1, LANES), F32)],
        compiler_params=_cparams(("arbitrary",)),
        name="router",
    )(hp, hs, g, wrt, br)


def _gather_rows_kernel(src_ref, tok_hbm, out_ref, buf, sem, *, tm, depth):
    i = pl.program_id(0)
    n = pl.num_programs(0)

    def tile_copy(slot):
        return pltpu.make_async_copy(tok_hbm.at[pl.ds(0, tm)], buf.at[slot], sem.at[slot])

    def issue(tile):
        slot = tile % depth

        def one(r, carry):
            t = src_ref[tile * tm + r]
            pltpu.make_async_copy(tok_hbm.at[t], buf.at[slot, r], sem.at[slot]).start()
            return carry
        lax.fori_loop(0, tm, one, 0, unroll=8)

    @pl.when(i == 0)
    def _():
        for ahead in range(depth - 1):
            @pl.when(ahead < n)
            def _(ahead=ahead):
                issue(ahead)

    @pl.when(i + depth - 1 < n)
    def _():
        issue(i + depth - 1)

    slot = i % depth
    tile_copy(slot).wait()
    out_ref[...] = buf[slot].reshape(out_ref.shape).astype(BF16)


def _gather_rows(src, tok3, n_rows, tm, depth=3):
    _, dg, lanes = tok3.shape
    return pl.pallas_call(
        functools.partial(_gather_rows_kernel, tm=tm, depth=depth),
        grid_spec=pltpu.PrefetchScalarGridSpec(
            num_scalar_prefetch=1,
            grid=(n_rows // tm,),
            in_specs=[pl.BlockSpec(memory_space=pl.ANY)],
            out_specs=pl.BlockSpec((tm, dg * lanes), lambda i, src_ref: (i, 0)),
            scratch_shapes=[pltpu.VMEM((depth, tm, dg, lanes), F32), pltpu.SemaphoreType.DMA((depth,))]),
        out_shape=jax.ShapeDtypeStruct((n_rows, dg * lanes), BF16),
        compiler_params=_cparams(("arbitrary",)),
        name="moe_gather",
    )(src, tok3)


def _grouped_rows_kernel(we_ref, wrow_ref, wnp_ref, tail_ref, x_hbm, *refs, n_w, col_blocks, grid, tn, tile_fn):
    n_tiles = len(MOE_TILE_PIECES)
    w_hbm = refs[:n_w]
    b_refs = refs[n_w:2 * n_w]
    out_hbm = refs[2 * n_w]
    scratch = refs[2 * n_w + 1:]
    xbuf = scratch[0]
    obufs = scratch[1:1 + n_tiles]
    zbuf = scratch[1 + n_tiles]
    wbufs = scratch[2 + n_tiles:2 + n_tiles + n_w]
    xsem, osem, zsem, wsem, cnt_ref = scratch[2 + n_tiles + n_w:]
    n_work, nj = grid
    n_steps = n_work * nj
    wi = pl.program_id(0)
    j = pl.program_id(1)
    step = wi * nj + j
    n_pieces = wnp_ref[wi]
    row0 = wrow_ref[wi]
    is_first = step == 0
    is_last = step == n_steps - 1

    @pl.when(is_first)
    def _():
        for ci in range(n_tiles):
            cnt_ref[ci] = 0

    def weight_copy(a, s_wi, s_j, slot):
        col = pl.multiple_of((col_blocks[a] + s_j) * tn, tn)
        return pltpu.make_async_copy(w_hbm[a].at[we_ref[s_wi], :, pl.ds(col, tn)], wbufs[a].at[slot],
                                     wsem.at[a, slot])

    def fetch(s):
        s_wi = s // nj
        s_j = s % nj

        @pl.when(wnp_ref[s_wi] > 0)
        def _():
            for a in range(n_w):
                weight_copy(a, s_wi, s_j, s % W_RING).start()

    @pl.when(is_first)
    def _():
        for ahead in range(min(W_RING - 1, n_steps)):
            fetch(ahead)

    @pl.when(step + (W_RING - 1) < n_steps)
    def _():
        fetch(step + (W_RING - 1))

    w_slot = step % W_RING
    w_views = [wb.at[w_slot] for wb in wbufs]

    @pl.when(n_pieces > 0)
    def _():
        for a in range(n_w):
            weight_copy(a, wi, j, w_slot).wait()

    def piece_copy(p):
        return pltpu.make_async_copy(x_hbm.at[pl.ds(pl.multiple_of(row0 + p * MOE_PIECE, MOE_PIECE), MOE_PIECE)],
                                     xbuf.at[pl.ds(pl.multiple_of(p * MOE_PIECE, MOE_PIECE), MOE_PIECE)],
                                     xsem.at[0])

    @pl.when(j == 0)
    def _():
        def start(p, c):
            piece_copy(p).start()
            return c

        def wait(p, c):
            piece_copy(p).wait()
            return c
        lax.fori_loop(0, n_pieces, start, 0)
        lax.fori_loop(0, n_pieces, wait, 0)

    col0 = pl.multiple_of(j * tn, tn)

    def out_copy(ci, slot, r0, rows):
        dst = out_hbm.at[pl.ds(pl.multiple_of(row0 + r0, MOE_PIECE), rows), pl.ds(col0, tn)]
        return pltpu.make_async_copy(obufs[ci].at[slot], dst, osem.at[ci, slot])

    def emit(ci, r0):
        rows = MOE_TILE_PIECES[ci] * MOE_PIECE
        r0 = pl.multiple_of(r0, MOE_PIECE)
        val = tile_fn(xbuf[pl.ds(r0, rows), :], w_views, b_refs)
        c = cnt_ref[ci]
        slot = c % 2

        @pl.when(c >= 2)
        def _():
            out_copy(ci, slot, r0, rows).wait()

        obufs[ci][slot] = val.astype(obufs[ci].dtype)
        out_copy(ci, slot, r0, rows).start()
        cnt_ref[ci] = c + 1

    remaining = n_pieces
    base = 0
    for ci, tp in enumerate(MOE_TILE_PIECES):
        n_t = remaining // tp

        def body(t, c, ci=ci, base=base, tp=tp):
            emit(ci, base + t * (tp * MOE_PIECE))
            return c
        lax.fori_loop(0, n_t, body, 0)
        base = base + n_t * (tp * MOE_PIECE)
        remaining = remaining - n_t * tp

    @pl.when(is_last)
    def _():
        for ci, tp in enumerate(MOE_TILE_PIECES):
            c = cnt_ref[ci]
            for back in (1, 2):
                @pl.when(c >= back)
                def _(ci=ci, back=back, c=c, tp=tp):
                    out_copy(ci, (c - back) % 2, 0, tp * MOE_PIECE).wait()
        zbuf[...] = jnp.zeros(zbuf.shape, zbuf.dtype)
        first_tail = tail_ref[0]
        n_total = out_hbm.shape[0] // MOE_PIECE

        def zcopy(p):
            return pltpu.make_async_copy(zbuf, out_hbm.at[pl.ds(pl.multiple_of(p * MOE_PIECE, MOE_PIECE), MOE_PIECE)],
                                         zsem.at[0])

        def zstart(p, c):
            zcopy(p).start()
            return c

        def zwait(p, c):
            zcopy(p).wait()
            return c
        lax.fori_loop(first_tail, n_total, zstart, 0)
        lax.fori_loop(first_tail, n_total, zwait, 0)


def _up_tile(x, w_views, b_refs):
    g = jnp.dot(x, w_views[0][...].astype(BF16), preferred_element_type=F32) + b_refs[0][0]
    up = jnp.dot(x, w_views[1][...].astype(BF16), preferred_element_type=F32) + b_refs[1][0]
    g = jnp.minimum(g, SWIGLU_LIMIT)
    up = jnp.clip(up, -SWIGLU_LIMIT, SWIGLU_LIMIT)
    return (up + 1.0) * (g * _sigmoid(SWIGLU_ALPHA * g))


def _down_tile(x, w_views, b_refs):
    return jnp.dot(x, w_views[0][...].astype(BF16), preferred_element_type=F32) + b_refs[0][0]


def _grouped_rows(tables, xs, weights, col_blocks, biases, n_out, tn, out_dtype, tile_fn, name):
    we, wrow, wnp, tail = tables
    n_rows, k = xs.shape
    n_w = len(weights)
    n_tiles = len(MOE_TILE_PIECES)
    grid = (we.shape[0], n_out // tn)
    b_specs = [pl.BlockSpec((1, 1, tn), lambda wi, j, we_ref, *_, cb=cb: (we_ref[wi], 0, cb + j)) for cb in col_blocks]
    scratch = [pltpu.VMEM((MOE_CAP_PIECES * MOE_PIECE, k), BF16)]
    scratch += [pltpu.VMEM((2, tp * MOE_PIECE, tn), out_dtype) for tp in MOE_TILE_PIECES]
    scratch += [pltpu.VMEM((MOE_PIECE, n_out), out_dtype)]
    scratch += [pltpu.VMEM((W_RING, k, tn), F32) for _ in weights]
    scratch += [pltpu.SemaphoreType.DMA((1,)),
                pltpu.SemaphoreType.DMA((n_tiles, 2)),
                pltpu.SemaphoreType.DMA((1,)),
                pltpu.SemaphoreType.DMA((n_w, W_RING)),
                pltpu.SMEM((n_tiles,), I32)]
    return pl.pallas_call(
        functools.partial(_grouped_rows_kernel, n_w=n_w, col_blocks=tuple(col_blocks), grid=grid, tn=tn,
                          tile_fn=tile_fn),
        grid_spec=pltpu.PrefetchScalarGridSpec(
            num_scalar_prefetch=4,
            grid=grid,
            in_specs=[pl.BlockSpec(memory_space=pl.ANY)] * (1 + n_w) + b_specs,
            out_specs=pl.BlockSpec(memory_space=pl.ANY),
            scratch_shapes=scratch),
        out_shape=jax.ShapeDtypeStruct((n_rows, n_out), out_dtype),
        compiler_params=_cparams(("arbitrary", "arbitrary")),
        name=name,
    )(we, wrow, wnp, tail, xs, *weights, *biases)


def _moe_up(tables, xs, w_gate_up, b_gate_up, tf=256):
    n_exp, d, two_f = w_gate_up.shape
    f = two_f // 2
    b3 = b_gate_up.reshape(n_exp, 1, two_f)
    return _grouped_rows(tables, xs, [w_gate_up, w_gate_up], [0, f // tf], [b3, b3], f, tf, BF16, _up_tile,
                         "moe_up")


def _moe_down(tables, act, w_down, b_down, tn=512):
    n_exp, f, d = w_down.shape
    return _grouped_rows(tables, act, [w_down], [0], [b_down.reshape(n_exp, 1, d)], d, tn, F32, _down_tile,
                         "moe_down")


def _combine_kernel(dest_ref, o_hbm, gate_ref, hp_ref, hs_ref, g_ref, yp_ref, ys_ref, buf, sem, *, tm,
                    n_prompt_tiles):
    i = pl.program_id(0)
    n = pl.num_programs(0)

    def issue(tile, slot):
        def one(r, carry):
            for kk in range(TOP_K):
                p = dest_ref[(tile * tm + r) * TOP_K + kk]
                pltpu.make_async_copy(o_hbm.at[pl.ds(p, 1), :], buf.at[slot, kk, pl.ds(r, 1), :],
                                      sem.at[slot]).start()
            return carry
        lax.fori_loop(0, tm, one, 0, unroll=2)

    @pl.when(i == 0)
    def _():
        issue(0, 0)

    @pl.when(i + 1 < n)
    def _():
        issue(i + 1, (i + 1) % 2)

    slot = i % 2
    for kk in range(TOP_K):
        pltpu.make_async_copy(o_hbm.at[pl.ds(0, tm), :], buf.at[slot, kk], sem.at[slot]).wait()

    gates = gate_ref[...]

    def finish(h_ref, y_ref):
        acc = h_ref[...]
        for kk in range(TOP_K):
            acc = acc + gates[:, kk:kk + 1] * buf[slot, kk]
        y_ref[...] = _rms(acc, g_ref[...])

    @pl.when(i < n_prompt_tiles)
    def _():
        finish(hp_ref, yp_ref)

    @pl.when(i >= n_prompt_tiles)
    def _():
        finish(hs_ref, ys_ref)


def _combine(dest_flat, o_sorted, gates, hp, hs, g_final):
    tp, d = hp.shape
    ts = hs.shape[0]
    tm = min(128, ts)
    assert tp % tm == 0 and ts % tm == 0
    npt, nst = tp // tm, ts // tm
    p_blk = lambda: pl.BlockSpec((tm, d), lambda i, dref: (jnp.minimum(i, npt - 1), 0))
    s_blk = lambda: pl.BlockSpec((tm, d), lambda i, dref: (jnp.maximum(i - npt, 0), 0))
    return pl.pallas_call(
        functools.partial(_combine_kernel, tm=tm, n_prompt_tiles=npt),
        grid_spec=pltpu.PrefetchScalarGridSpec(
            num_scalar_prefetch=1,
            grid=(npt + nst,),
            in_specs=[pl.BlockSpec(memory_space=pl.ANY),
                      pl.BlockSpec((tm, LANES), lambda i, dref: (i, 0)),
                      p_blk(), s_blk(),
                      pl.BlockSpec((1, d), lambda i, dref: (0, 0))],
            out_specs=[p_blk(), s_blk()],
            scratch_shapes=[pltpu.VMEM((2, TOP_K, tm, d), F32), pltpu.SemaphoreType.DMA((2,))]),
        out_shape=[jax.ShapeDtypeStruct((tp, d), F32), jax.ShapeDtypeStruct((ts, d), F32)],
        compiler_params=_cparams(("arbitrary",)),
        name="moe_combine",
    )(dest_flat, o_sorted, gates, hp, hs, g_final.reshape(1, d))


def _mixer_front(x2, g_mix, w_in_t, b_i, b_f, n_heads, n_main):
    d = x2.shape[1]
    wgt = jnp.pad(w_in_t[n_main:], ((0, LANES - 2 * n_heads), (0, 0)))
    b_gate = jnp.pad(jnp.concatenate([b_i, b_f]), (0, LANES - 2 * n_heads)).reshape(1, LANES)
    xn, gates = _norm_gates(x2, g_mix.reshape(1, d), wgt, b_gate)
    u = _proj([xn], w_in_t, n_main, w_is_nk=True)
    return u, gates


def _moe(hp, hs, g_ffn, w_router_t, b_router, w_gate_up, b_gate_up, w_down, b_down, g_final):
    tp, d = hp.shape
    ts = hs.shape[0]
    t = tp + ts
    n_exp = w_router_t.shape[0]
    wrt = jnp.pad(w_router_t, ((0, LANES - n_exp), (0, 0)))
    br = jnp.pad(b_router, (0, LANES - n_exp), constant_values=NEG_BIG).reshape(1, LANES)
    tok3, idx_o, gate_o, rank_o, cnt_o = _router(hp, hs, g_ffn.reshape(1, d), wrt, br, tm=min(256, ts))

    n_rows = _round_up(t * TOP_K + n_exp * (MOE_PIECE - 1), GATHER_TILE)
    n_work = n_exp + (n_rows // MOE_PIECE) // MOE_CAP_PIECES
    counts = cnt_o[0, :n_exp]
    pieces = (counts + MOE_PIECE - 1) // MOE_PIECE
    first_piece = jnp.cumsum(pieces) - pieces
    chunks = (pieces + MOE_CAP_PIECES - 1) // MOE_CAP_PIECES
    cum_chunks = jnp.cumsum(chunks)
    wid = jnp.arange(n_work, dtype=I32)
    used = wid < cum_chunks[-1]
    e_of_w = jnp.sum((cum_chunks[None, :] <= wid[:, None]).astype(I32), axis=1)
    last_e = jnp.max(jnp.where(chunks > 0, jnp.arange(n_exp, dtype=I32), 0))
    we = jnp.where(used, jnp.minimum(e_of_w, n_exp - 1), last_e).astype(I32)
    chunk_in_e = wid - (cum_chunks - chunks)[we]
    wrow = jnp.where(used, (first_piece[we] + chunk_in_e * MOE_CAP_PIECES) * MOE_PIECE, 0).astype(I32)
    wnp = jnp.where(used, jnp.clip(pieces[we] - chunk_in_e * MOE_CAP_PIECES, 0, MOE_CAP_PIECES), 0).astype(I32)
    tail = jnp.sum(pieces).reshape(1).astype(I32)
    tables = (we, wrow, wnp, tail)

    dest = (first_piece[idx_o[:, :TOP_K]] * MOE_PIECE + rank_o[:, :TOP_K]).reshape(-1)
    token_of_pair = jnp.repeat(jnp.arange(t, dtype=I32), TOP_K)
    src = jnp.zeros((n_rows,), I32).at[dest].set(token_of_pair)

    xs = _gather_rows(src, tok3, n_rows, GATHER_TILE)
    act = _moe_up(tables, xs, w_gate_up, b_gate_up)
    o_sorted = _moe_down(tables, act, w_down, b_down)
    return _combine(dest, o_sorted, gate_o, hp, hs, g_final)


def kernel(x_prompt, x_sample, state_conv, state_C, state_n, state_m, g_mix, w_in, b_i, b_f, w_dw, b_dw,
           conv_ln_g, conv_ln_b, mlstm_norm_g, w_out, g_ffn, w_router, b_router, w_gate_up, b_gate_up,
           w_down, b_down, g_final):
    b, s, d = x_prompt.shape
    bd, sd, _ = x_sample.shape
    depth = w_in.shape[0]
    assert depth == 1, "the final norm is fused into the MoE combine of the single layer"
    n_heads, dk, dv = state_C.shape[2], state_C.shape[3], state_C.shape[4]
    c_conv = state_conv.shape[3]
    n_main = 2 * c_conv + 2 * n_heads * dk + 2 * n_heads * dv
    l = 0
    w_in_t = jnp.swapaxes(w_in[l], 0, 1)
    w_router_t = jnp.swapaxes(w_router[l], 0, 1)

    xp2 = x_prompt.reshape(b * s, d)
    up, gates_p = _mixer_front(xp2, g_mix[l], w_in_t, b_i[l], b_f[l], n_heads, n_main)
    up3 = up.reshape(b, s, n_main)
    conv_p, cbp = _conv_seq(up3, w_dw[l], b_dw[l], conv_ln_g[l], conv_ln_b[l], c_conv)
    hm_p, c_p, n_p, m_p = _mlstm(up3, gates_p.reshape(b, s, LANES), mlstm_norm_g[l], None, n_heads, dk, dv, c_conv)
    hp = _proj([conv_p.reshape(b * s, c_conv), hm_p.reshape(b * s, n_heads * dv)], w_out[l], d, res=xp2)

    xs2 = x_sample.reshape(bd * sd, d)
    us, gates_s = _mixer_front(xs2, g_mix[l], w_in_t, b_i[l], b_f[l], n_heads, n_main)
    us3 = us.reshape(bd, sd, n_main)
    ug = jnp.transpose(us3[:, :, :2 * c_conv], (1, 0, 2))
    state_t = jnp.transpose(state_conv[l], (1, 0, 2))
    conv_s_t, cbs_t = _conv_step(ug, state_t, w_dw[l], b_dw[l], conv_ln_g[l], conv_ln_b[l], c_conv)
    conv_s = jnp.transpose(conv_s_t, (1, 0, 2)).reshape(bd * sd, c_conv)
    cbs = jnp.transpose(cbs_t, (1, 0, 2))
    hm_s, c_s, n_s, m_s = _mlstm(us3, gates_s.reshape(bd, sd, LANES), mlstm_norm_g[l],
                                 (state_C[l], state_n[l], state_m[l]), n_heads, dk, dv, c_conv)
    hs = _proj([conv_s, hm_s.reshape(bd * sd, n_heads * dv)], w_out[l], d, res=xs2)

    yp, ys = _moe(hp, hs, g_ffn[l], w_router_t, b_router[l], w_gate_up[l], b_gate_up[l], w_down[l], b_down[l],
                  g_final)
    return (yp.reshape(b, s, d), ys.reshape(bd, sd, d), cbp[None], c_p[None], n_p[None], m_p[None],
            cbs[None], c_s[None], n_s[None], m_s[None])
```

```python
import functools

import jax
import jax.numpy as jnp
from jax import lax
from jax.experimental import pallas as pl
from jax.experimental.pallas import tpu as pltpu

F32 = jnp.float32
BF16 = jnp.bfloat16
I32 = jnp.int32

NORM_EPS = 1e-5
TOP_K = 4
SWIGLU_LIMIT = 7.0
SWIGLU_ALPHA = 1.702
LANES = 128
SUBLANES = 8
VMEM_LIMIT_BYTES = 56 * 1024 * 1024
NEG_BIG = -1e30

MOE_PIECE = 128
MOE_CAP_PIECES = 16
W_RING = 3
GATHER_TILE = 256


def _round_up(a, b):
    return (a + b - 1) // b * b


def _cparams(sem):
    return pltpu.CompilerParams(dimension_semantics=sem, vmem_limit_bytes=VMEM_LIMIT_BYTES)


def _sigmoid(x):
    return 1.0 / (1.0 + jnp.exp(-x))


def _log_sigmoid(x):
    return jnp.minimum(x, 0.0) - jnp.log(1.0 + jnp.exp(-jnp.abs(x)))


def _rms(x, g):
    return x * lax.rsqrt(jnp.mean(x * x, axis=-1, keepdims=True) + NORM_EPS) * g


_NT = (((1,), (1,)), ((), ()))


def _norm_gates_kernel(x_ref, g_ref, wgt_ref, bg_ref, xn_ref, gt_ref):
    yb = _rms(x_ref[...], g_ref[...]).astype(BF16)
    xn_ref[...] = yb
    gt_ref[...] = lax.dot_general(yb, wgt_ref[...].astype(BF16), _NT, preferred_element_type=F32) + bg_ref[...]


def _norm_gates(x, g, wgt, bg):
    tn, d = x.shape
    tm = min(512, tn)
    return pl.pallas_call(
        _norm_gates_kernel,
        grid=(tn // tm,),
        in_specs=[pl.BlockSpec((tm, d), lambda i: (i, 0)),
                  pl.BlockSpec((1, d), lambda i: (0, 0)),
                  pl.BlockSpec((LANES, d), lambda i: (0, 0)),
                  pl.BlockSpec((1, LANES), lambda i: (0, 0))],
        out_specs=[pl.BlockSpec((tm, d), lambda i: (i, 0)),
                   pl.BlockSpec((tm, LANES), lambda i: (i, 0))],
        out_shape=[jax.ShapeDtypeStruct((tn, d), BF16), jax.ShapeDtypeStruct((tn, LANES), F32)],
        compiler_params=_cparams(("arbitrary",)),
        name="norm_gates",
    )(x, g, wgt, bg)


def _proj_kernel(*refs, n_lhs, has_res, w_is_nk):
    lhs_refs = refs[:n_lhs]
    w_ref = refs[n_lhs]
    res_ref = refs[n_lhs + 1] if has_res else None
    o_ref = refs[-1]
    acc = None
    k0 = 0
    for r in lhs_refs:
        kk = r.shape[1]
        lhs = r[...].astype(BF16)
        if w_is_nk:
            part = lax.dot_general(lhs, w_ref[:, k0:k0 + kk].astype(BF16), _NT, preferred_element_type=F32)
        else:
            part = jnp.dot(lhs, w_ref[k0:k0 + kk, :].astype(BF16), preferred_element_type=F32)
        acc = part if acc is None else acc + part
        k0 += kk
    if has_res:
        acc = acc + res_ref[...]
    o_ref[...] = acc


def _proj(lhs_list, w, n_out, res=None, w_is_nk=False, tn=512):
    rows = lhs_list[0].shape[0]
    k_total = w.shape[1] if w_is_nk else w.shape[0]
    assert sum(a.shape[1] for a in lhs_list) == k_total and n_out % tn == 0
    tm = min(1024, rows)
    in_specs = [pl.BlockSpec((tm, a.shape[1]), lambda j, i: (i, 0)) for a in lhs_list]
    if w_is_nk:
        in_specs.append(pl.BlockSpec((tn, k_total), lambda j, i: (j, 0)))
    else:
        in_specs.append(pl.BlockSpec((k_total, tn), lambda j, i: (0, j)))
    args = list(lhs_list) + [w]
    if res is not None:
        in_specs.append(pl.BlockSpec((tm, tn), lambda j, i: (i, j)))
        args.append(res)
    return pl.pallas_call(
        functools.partial(_proj_kernel, n_lhs=len(lhs_list), has_res=res is not None, w_is_nk=w_is_nk),
        grid=(n_out // tn, rows // tm),
        in_specs=in_specs,
        out_specs=pl.BlockSpec((tm, tn), lambda j, i: (i, j)),
        out_shape=jax.ShapeDtypeStruct((rows, n_out), F32),
        compiler_params=_cparams(("arbitrary", "arbitrary")),
        name="proj",
    )(*args)


def _layernorm_swish(y, g, b):
    mu = jnp.mean(y, axis=-1, keepdims=True)
    yc = y - mu
    var = jnp.mean(yc * yc, axis=-1, keepdims=True)
    z = yc * lax.rsqrt(var + NORM_EPS) * g + b
    return z * _sigmoid(z)


def _conv_seq_kernel(ua_ref, ub_ref, wdw_ref, bdw_ref, lng_ref, lnb_ref, out_ref, nbuf_ref, full_ref, y_ref,
                     *, tl, kw, cc):
    halo = kw - 1
    padh = _round_up(halo, SUBLANES)
    off = padh - halo
    c = ua_ref.shape[-1]
    l = pl.program_id(1)

    @pl.when(l == 0)
    def _():
        full_ref[off:padh, :] = jnp.zeros((halo, c), F32)
        full_ref[padh + tl:, :] = jnp.zeros((SUBLANES, c), F32)

    full_ref[padh:padh + tl, :] = ua_ref[0] * _sigmoid(ub_ref[0])

    by_residue = {}
    for w in range(kw):
        by_residue.setdefault((off + w) % SUBLANES, []).append(w)
    for c0 in range(0, c, cc):
        acc = jnp.zeros((tl, cc), F32) + bdw_ref[:, c0:c0 + cc]
        for r, taps in sorted(by_residue.items()):
            z = None
            for w in taps:
                base = off + w - r
                term = full_ref[base:base + tl + SUBLANES, c0:c0 + cc] * wdw_ref[w:w + 1, c0:c0 + cc]
                z = term if z is None else z + term
            acc = acc + z[r:r + tl]
        y_ref[:, c0:c0 + cc] = acc

    out_ref[0] = _layernorm_swish(y_ref[...], lng_ref[...], lnb_ref[...]).astype(BF16)

    tail = full_ref[tl + off:tl + padh, :]

    @pl.when(l == pl.num_programs(1) - 1)
    def _():
        nbuf_ref[0] = tail

    full_ref[off:padh, :] = tail


def _conv_seq(u3, w_dw, b_dw, ln_g, ln_b, c):
    b, l, _ = u3.shape
    kw = w_dw.shape[0]
    halo = kw - 1
    tl = 64
    assert l % tl == 0
    padh = _round_up(halo, SUBLANES)
    return pl.pallas_call(
        functools.partial(_conv_seq_kernel, tl=tl, kw=kw, cc=min(256, c)),
        grid=(b, l // tl),
        in_specs=[pl.BlockSpec((1, tl, c), lambda bi, li: (bi, li, 0)),
                  pl.BlockSpec((1, tl, c), lambda bi, li: (bi, li, 1)),
                  pl.BlockSpec((kw, c), lambda bi, li: (0, 0))] + [pl.BlockSpec((1, c), lambda bi, li: (0, 0))] * 3,
        out_specs=[pl.BlockSpec((1, tl, c), lambda bi, li: (bi, li, 0)),
                   pl.BlockSpec((1, halo, c), lambda bi, li: (bi, 0, 0))],
        out_shape=[jax.ShapeDtypeStruct((b, l, c), BF16), jax.ShapeDtypeStruct((b, halo, c), F32)],
        scratch_shapes=[pltpu.VMEM((padh + tl + SUBLANES, c), F32), pltpu.VMEM((tl, c), F32)],
        compiler_params=_cparams(("arbitrary", "arbitrary")),
        name="conv_seq",
    )(u3, u3, w_dw, b_dw.reshape(1, c), ln_g.reshape(1, c), ln_b.reshape(1, c))


def _conv_step_kernel(ua_ref, ub_ref, st_ref, wdw_ref, bdw_ref, lng_ref, lnb_ref, out_ref, nst_ref, y_ref,
                      *, ls, kw, cc):
    halo = kw - 1
    j = pl.program_id(0)
    glu = [ua_ref[t] * _sigmoid(ub_ref[t]) for t in range(ls)]

    def window(idx):
        return st_ref[idx] if idx < halo else glu[idx - halo]

    for t in range(ls):
        acc = jnp.zeros(glu[0].shape, F32) + bdw_ref[...]
        for w in range(kw):
            acc = acc + window(t + w) * wdw_ref[w:w + 1, :]
        y_ref[j, t] = acc
    for idx in range(halo):
        nst_ref[idx] = window(ls + idx)

    @pl.when(j == pl.num_programs(0) - 1)
    def _():
        for t in range(ls):
            y = jnp.concatenate([y_ref[jj, t] for jj in range(y_ref.shape[0])], axis=-1)
            out_ref[t] = _layernorm_swish(y, lng_ref[...], lnb_ref[...]).astype(BF16)


def _conv_step(ug, state_t, w_dw, b_dw, ln_g, ln_b, c):
    ls, b, _ = ug.shape
    kw = w_dw.shape[0]
    halo = kw - 1
    cc = min(256, c)
    nj = c // cc
    return pl.pallas_call(
        functools.partial(_conv_step_kernel, ls=ls, kw=kw, cc=cc),
        grid=(nj,),
        in_specs=[pl.BlockSpec((ls, b, cc), lambda j: (0, 0, j)),
                  pl.BlockSpec((ls, b, cc), lambda j: (0, 0, nj + j)),
                  pl.BlockSpec((halo, b, cc), lambda j: (0, 0, j)),
                  pl.BlockSpec((kw, cc), lambda j: (0, j)),
                  pl.BlockSpec((1, cc), lambda j: (0, j)),
                  pl.BlockSpec((1, c), lambda j: (0, 0)),
                  pl.BlockSpec((1, c), lambda j: (0, 0))],
        out_specs=[pl.BlockSpec((ls, b, c), lambda j: (0, 0, 0)),
                   pl.BlockSpec((halo, b, cc), lambda j: (0, 0, j))],
        out_shape=[jax.ShapeDtypeStruct((ls, b, c), BF16), jax.ShapeDtypeStruct((halo, b, c), F32)],
        scratch_shapes=[pltpu.VMEM((nj, ls, b, cc), F32)],
        compiler_params=_cparams(("arbitrary",)),
        name="conv_step",
    )(ug, ug, state_t, w_dw, b_dw.reshape(1, c), ln_g.reshape(1, c), ln_b.reshape(1, c))


def _mlstm_kernel(*refs, lc, dk, dv, n_heads, has_state):
    if has_state:
        (q_ref, k_ref, v_ref, o_ref, gc_ref, gr_ref, ng_ref, c0_ref, n0_ref, m0_ref,
         hm_ref, co_ref, no_ref, mo_ref, c_s, n_s, m_s) = refs
    else:
        (q_ref, k_ref, v_ref, o_ref, gc_ref, gr_ref, ng_ref,
         hm_ref, co_ref, no_ref, mo_ref, c_s, n_s, m_s) = refs
    j = pl.program_id(1)

    @pl.when(j == 0)
    def _():
        if has_state:
            c_s[...] = c0_ref[0]
            n_s[...] = n0_ref[0]
            m_s[...] = m0_ref[0]
        else:
            c_s[...] = jnp.zeros(c_s.shape, F32)
            n_s[...] = jnp.zeros(n_s.shape, F32)
            m_s[...] = jnp.zeros(m_s.shape, F32)

    gc = gc_ref[0]
    gr = gr_ref[0]
    ri = lax.broadcasted_iota(I32, (lc, lc), 0)
    ci = lax.broadcasted_iota(I32, (lc, lc), 1)
    causal = ri >= ci

    for h in range(n_heads):
        q = q_ref[0, :, h * dk:(h + 1) * dk] * (dk ** -0.5)
        kf = k_ref[0, :, h * dk:(h + 1) * dk]
        qb = q.astype(BF16)
        kb = kf.astype(BF16)
        vb = v_ref[0, :, h * dv:(h + 1) * dv].astype(BF16)

        gi_col = gc[:, h:h + 1]
        gf_col = gc[:, n_heads + h:n_heads + h + 1]
        gi_row = gr[h:h + 1, :]
        gf_row = gr[n_heads + h:n_heads + h + 1, :]
        lf_col = _log_sigmoid(gf_col)
        lf_row = _log_sigmoid(gf_row)
        a_col = jnp.sum(jnp.where(causal, lf_row, 0.0), axis=1, keepdims=True)
        a_row = jnp.sum(jnp.where(ri <= ci, lf_col, 0.0), axis=0, keepdims=True)

        m_prev = m_s[h]
        d_log = jnp.where(causal, a_col - a_row + gi_row, -jnp.inf)
        inter = a_col + m_prev
        m_row = jnp.maximum(inter, jnp.max(d_log, axis=1, keepdims=True))
        w_intra = jnp.exp(d_log - m_row)
        w_inter = jnp.exp(inter - m_row)

        s = lax.dot_general(qb, kb, _NT, preferred_element_type=F32) * w_intra
        c_prev = c_s[h]
        n_prev = n_s[h]
        q_c = jnp.dot(qb, c_prev.astype(BF16), preferred_element_type=F32)
        num = jnp.dot(s.astype(BF16), vb, preferred_element_type=F32) + w_inter * q_c
        q_n = jnp.sum(q * n_prev, axis=1, keepdims=True)
        den_raw = jnp.sum(s, axis=1, keepdims=True) + w_inter * q_n
        den = jnp.maximum(jnp.abs(den_raw), jnp.exp(-m_row))
        hh = num / den

        hn = _rms(hh, ng_ref[:, h * dv:(h + 1) * dv])
        hm_ref[0, :, h * dv:(h + 1) * dv] = (hn * _sigmoid(o_ref[0, :, h * dv:(h + 1) * dv])).astype(BF16)

        a_end = a_col[lc - 1:lc, :]
        m_new = m_row[lc - 1:lc, :]
        decay = jnp.exp(a_end + m_prev - m_new)
        w_end = jnp.exp(a_end - a_col + gi_col - m_new)
        kw = kf * w_end
        c_new = decay * c_prev + lax.dot_general(kw.astype(BF16), vb, (((0,), (0,)), ((), ())),
                                                 preferred_element_type=F32)
        n_new = decay * n_prev + jnp.sum(kw, axis=0, keepdims=True)
        c_s[h] = c_new
        n_s[h] = n_new
        m_s[h] = m_new

    @pl.when(j == pl.num_programs(1) - 1)
    def _():
        co_ref[0] = c_s[...]
        no_ref[0] = n_s[...]
        mo_ref[0] = m_s[...]


def _mlstm(u3, gates3, norm_g, state, n_heads, dk, dv, c_conv):
    b, l, _ = u3.shape
    lc = 256 if l % 256 == 0 else l
    nc = l // lc
    wq, wv = n_heads * dk, n_heads * dv
    assert (2 * c_conv) % wq == 0 and (2 * c_conv + 2 * wq) % wv == 0
    q0 = 2 * c_conv // wq
    v0 = (2 * c_conv + 2 * wq) // wv
    n_gate_rows = _round_up(2 * n_heads, SUBLANES)
    gates_row = jnp.transpose(gates3[:, :, :n_gate_rows], (0, 2, 1))
    has_state = state is not None
    in_specs = [pl.BlockSpec((1, lc, wq), lambda bi, ji: (bi, ji, q0)),
                pl.BlockSpec((1, lc, wq), lambda bi, ji: (bi, ji, q0 + 1)),
                pl.BlockSpec((1, lc, wv), lambda bi, ji: (bi, ji, v0)),
                pl.BlockSpec((1, lc, wv), lambda bi, ji: (bi, ji, v0 + 1)),
                pl.BlockSpec((1, lc, LANES), lambda bi, ji: (bi, ji, 0)),
                pl.BlockSpec((1, n_gate_rows, lc), lambda bi, ji: (bi, 0, ji)),
                pl.BlockSpec((1, wv), lambda bi, ji: (0, 0))]
    args = [u3, u3, u3, u3, gates3, gates_row, norm_g.reshape(1, wv)]
    state_specs = lambda: [pl.BlockSpec((1, n_heads, dk, dv), lambda bi, ji: (bi, 0, 0, 0)),
                           pl.BlockSpec((1, n_heads, 1, dk), lambda bi, ji: (bi, 0, 0, 0)),
                           pl.BlockSpec((1, n_heads, 1, 1), lambda bi, ji: (bi, 0, 0, 0))]
    if has_state:
        c0, n0, m0 = state
        in_specs += state_specs()
        args += [c0, n0.reshape(b, n_heads, 1, dk), m0.reshape(b, n_heads, 1, 1)]
    hm, c_f, n_f, m_f = pl.pallas_call(
        functools.partial(_mlstm_kernel, lc=lc, dk=dk, dv=dv, n_heads=n_heads, has_state=has_state),
        grid=(b, nc),
        in_specs=in_specs,
        out_specs=[pl.BlockSpec((1, lc, wv), lambda bi, ji: (bi, ji, 0))] + state_specs(),
        out_shape=[jax.ShapeDtypeStruct((b, l, wv), BF16),
                   jax.ShapeDtypeStruct((b, n_heads, dk, dv), F32),
                   jax.ShapeDtypeStruct((b, n_heads, 1, dk), F32),
                   jax.ShapeDtypeStruct((b, n_heads, 1, 1), F32)],
        scratch_shapes=[pltpu.VMEM((n_heads, dk, dv), F32), pltpu.VMEM((n_heads, 1, dk), F32),
                        pltpu.VMEM((n_heads, 1, 1), F32)],
        compiler_params=_cparams(("arbitrary", "arbitrary")),
        name="mlstm",
    )(*args)
    return hm, c_f, n_f.reshape(b, n_heads, dk), m_f.reshape(b, n_heads)


def _router_kernel(hp_ref, hs_ref, g_ref, wrt_ref, br_ref, tok_ref, idx_ref, gate_ref, rank_ref, cnt_ref,
                   carry_ref, *, n_prompt_tiles):
    i = pl.program_id(0)

    @pl.when(i == 0)
    def _():
        carry_ref[...] = jnp.zeros(carry_ref.shape, F32)

    x = jnp.where(i < n_prompt_tiles, hp_ref[...], hs_ref[...])
    tok = _rms(x, g_ref[...])
    tok_ref[...] = tok.reshape(tok_ref.shape)
    lg = lax.dot_general(tok, wrt_ref[...], _NT, preferred_element_type=F32,
                         precision=lax.Precision.HIGHEST) + br_ref[...]
    tm = lg.shape[0]
    lane = lax.broadcasted_iota(I32, lg.shape, 1).astype(F32)
    vals, idxs = [], []
    for _ in range(TOP_K):
        mx = jnp.max(lg, axis=1, keepdims=True)
        ik = jnp.min(jnp.where(lg == mx, lane, float(LANES)), axis=1, keepdims=True)
        vals.append(mx)
        idxs.append(ik)
        lg = jnp.where(lane == ik, -jnp.inf, lg)
    exps = [jnp.exp(v - vals[0]) for v in vals]
    ssum = exps[0]
    for e in exps[1:]:
        ssum = ssum + e
    gates = [e / ssum for e in exps]

    multi = jnp.zeros(lg.shape, F32)
    for ik in idxs:
        multi = multi + (lane == ik).astype(F32)
    ri = lax.broadcasted_iota(I32, (tm, tm), 0)
    ci = lax.broadcasted_iota(I32, (tm, tm), 1)
    below = (ri > ci).astype(BF16)
    cum = jnp.dot(below, multi.astype(BF16), preferred_element_type=F32) + carry_ref[...]
    ranks = [jnp.sum(jnp.where(lane == ik, cum, 0.0), axis=1, keepdims=True) for ik in idxs]
    carry = carry_ref[...] + jnp.sum(multi, axis=0, keepdims=True)
    carry_ref[...] = carry
    cnt_ref[...] = carry.astype(I32)

    idx_o = jnp.zeros(lg.shape, F32)
    gate_o = jnp.zeros(lg.shape, F32)
    rank_o = jnp.zeros(lg.shape, F32)
    for kk in range(TOP_K):
        sel = lane == float(kk)
        idx_o = jnp.where(sel, idxs[kk], idx_o)
        gate_o = jnp.where(sel, gates[kk], gate_o)
        rank_o = jnp.where(sel, ranks[kk], rank_o)
    idx_ref[...] = idx_o.astype(I32)
    gate_ref[...] = gate_o
    rank_ref[...] = rank_o.astype(I32)


def _router(hp, hs, g, wrt, br, tm):
    tp, d = hp.shape
    ts = hs.shape[0]
    assert tp % tm == 0 and ts % tm == 0
    npt, nst = tp // tm, ts // tm
    t = tp + ts
    dg = d // LANES
    blk = lambda w: pl.BlockSpec((tm, w), lambda i: (i, 0))
    return pl.pallas_call(
        functools.partial(_router_kernel, n_prompt_tiles=npt),
        grid=(npt + nst,),
        in_specs=[pl.BlockSpec((tm, d), lambda i: (jnp.minimum(i, npt - 1), 0)),
                  pl.BlockSpec((tm, d), lambda i: (jnp.maximum(i - npt, 0), 0)),
                  pl.BlockSpec((1, d), lambda i: (0, 0)),
                  pl.BlockSpec((LANES, d), lambda i: (0, 0)),
                  pl.BlockSpec((1, LANES), lambda i: (0, 0))],
        out_specs=[pl.BlockSpec((tm, dg, LANES), lambda i: (i, 0, 0)), blk(LANES), blk(LANES), blk(LANES),
                   pl.BlockSpec((1, LANES), lambda i: (0, 0))],
        out_shape=[jax.ShapeDtypeStruct((t, dg, LANES), F32), jax.ShapeDtypeStruct((t, LANES), I32),
                   jax.ShapeDtypeStruct((t, LANES), F32), jax.ShapeDtypeStruct((t, LANES), I32),
                   jax.ShapeDtypeStruct((1, LANES), I32)],
        scratch_shapes=[pltpu.VMEM((1, LANES), F32)],
        compiler_params=_cparams(("arbitrary",)),
        name="router",
    )(hp, hs, g, wrt, br)


def _gather_rows_kernel(src_ref, tok_hbm, out_ref, buf, sem, *, tm, depth):
    i = pl.program_id(0)
    n = pl.num_programs(0)

    def tile_copy(slot):
        return pltpu.make_async_copy(tok_hbm.at[pl.ds(0, tm)], buf.at[slot], sem.at[slot])

    def issue(tile):
        slot = tile % depth

        def one(r, carry):
            t = src_ref[tile * tm + r]
            pltpu.make_async_copy(tok_hbm.at[t], buf.at[slot, r], sem.at[slot]).start()
            return carry
        lax.fori_loop(0, tm, one, 0, unroll=8)

    @pl.when(i == 0)
    def _():
        for ahead in range(depth - 1):
            @pl.when(ahead < n)
            def _(ahead=ahead):
                issue(ahead)

    @pl.when(i + depth - 1 < n)
    def _():
        issue(i + depth - 1)

    slot = i % depth
    tile_copy(slot).wait()
    out_ref[...] = buf[slot].reshape(out_ref.shape).astype(BF16)


def _gather_rows(src, tok3, n_rows, tm, depth=3):
    _, dg, lanes = tok3.shape
    return pl.pallas_call(
        functools.partial(_gather_rows_kernel, tm=tm, depth=depth),
        grid_spec=pltpu.PrefetchScalarGridSpec(
            num_scalar_prefetch=1,
            grid=(n_rows // tm,),
            in_specs=[pl.BlockSpec(memory_space=pl.ANY)],
            out_specs=pl.BlockSpec((tm, dg * lanes), lambda i, src_ref: (i, 0)),
            scratch_shapes=[pltpu.VMEM((depth, tm, dg, lanes), F32), pltpu.SemaphoreType.DMA((depth,))]),
        out_shape=jax.ShapeDtypeStruct((n_rows, dg * lanes), BF16),
        compiler_params=_cparams(("arbitrary",)),
        name="moe_gather",
    )(src, tok3)


def _grouped_rows_kernel(wrow_ref, wnp_ref, tail_ref, x_hbm, *refs, n_w, col_blocks, grid, tn, tile_pieces, tile_fn):
    n_tiles = len(tile_pieces)
    w_hbm = refs[:n_w]
    b_refs = refs[n_w:2 * n_w]
    out_hbm = refs[2 * n_w]
    scratch = refs[2 * n_w + 1:]
    xbuf = scratch[0]
    obufs = scratch[1:1 + n_tiles]
    zbuf = scratch[1 + n_tiles]
    wbuf = scratch[2 + n_tiles]
    xsem, osem, zsem, wsem, cnt_ref = scratch[3 + n_tiles:]
    n_work, nj = grid
    n_steps = n_work * nj
    wi = pl.program_id(0)
    j = pl.program_id(1)
    step = wi * nj + j
    n_pieces = wnp_ref[wi]
    row0 = wrow_ref[wi]
    is_first = step == 0
    is_last = step == n_steps - 1

    @pl.when(is_first)
    def _():
        for ci in range(n_tiles):
            cnt_ref[ci] = 0

    def weight_copy(a, s_wi, s_j, slot):
        col = pl.multiple_of((col_blocks[a] + s_j) * tn, tn)
        return pltpu.make_async_copy(w_hbm[a].at[s_wi, :, pl.ds(col, tn)],
                                     wbuf.at[slot, :, pl.ds(a * tn, tn)], wsem.at[a, slot])

    def fetch(s):
        s_wi = s // nj
        s_j = s % nj

        @pl.when(wnp_ref[s_wi] > 0)
        def _():
            for a in range(n_w):
                weight_copy(a, s_wi, s_j, s % W_RING).start()

    @pl.when(is_first)
    def _():
        for ahead in range(min(W_RING - 1, n_steps)):
            fetch(ahead)

    @pl.when(step + (W_RING - 1) < n_steps)
    def _():
        fetch(step + (W_RING - 1))

    w_slot = step % W_RING
    w_view = wbuf.at[w_slot]

    @pl.when(n_pieces > 0)
    def _():
        for a in range(n_w):
            weight_copy(a, wi, j, w_slot).wait()

    col0 = pl.multiple_of(j * tn, tn)
    cap_rows = xbuf.shape[0]
    cap_pieces = cap_rows // MOE_PIECE
    n_chunks = (n_pieces + cap_pieces - 1) // cap_pieces

    def out_copy(ci, slot, row, rows):
        dst = out_hbm.at[pl.ds(pl.multiple_of(row, MOE_PIECE), rows), pl.ds(col0, tn)]
        return pltpu.make_async_copy(obufs[ci].at[slot], dst, osem.at[ci, slot])

    def emit(ci, chunk_row0, r0):
        rows = tile_pieces[ci] * MOE_PIECE
        r0 = pl.multiple_of(r0, MOE_PIECE)
        val = tile_fn(xbuf[pl.ds(r0, rows), :], w_view, b_refs)
        c = cnt_ref[ci]
        slot = c % 2

        @pl.when(c >= 2)
        def _():
            out_copy(ci, slot, chunk_row0 + r0, rows).wait()

        obufs[ci][slot] = val.astype(obufs[ci].dtype)
        out_copy(ci, slot, chunk_row0 + r0, rows).start()
        cnt_ref[ci] = c + 1

    def chunk(ck, carry):
        chunk_row0 = row0 + ck * cap_rows
        chunk_pieces = jnp.minimum(n_pieces - ck * cap_pieces, cap_pieces)

        def piece_copy(p):
            off = pl.multiple_of(p * MOE_PIECE, MOE_PIECE)
            return pltpu.make_async_copy(x_hbm.at[pl.ds(pl.multiple_of(chunk_row0 + off, MOE_PIECE), MOE_PIECE)],
                                         xbuf.at[pl.ds(off, MOE_PIECE)], xsem.at[0])

        @pl.when(jnp.logical_or(j == 0, n_chunks > 1))
        def _():
            def start(p, c):
                piece_copy(p).start()
                return c

            def wait(p, c):
                piece_copy(p).wait()
                return c
            lax.fori_loop(0, chunk_pieces, start, 0)
            lax.fori_loop(0, chunk_pieces, wait, 0)

        remaining = chunk_pieces
        base = 0
        for ci, tp in enumerate(tile_pieces):
            n_t = remaining // tp

            def body(t, c, ci=ci, base=base, tp=tp):
                emit(ci, chunk_row0, base + t * (tp * MOE_PIECE))
                return c
            lax.fori_loop(0, n_t, body, 0)
            base = base + n_t * (tp * MOE_PIECE)
            remaining = remaining - n_t * tp
        return carry

    lax.fori_loop(0, n_chunks, chunk, 0)

    @pl.when(is_last)
    def _():
        for ci, tp in enumerate(tile_pieces):
            c = cnt_ref[ci]
            for back in (1, 2):
                @pl.when(c >= back)
                def _(ci=ci, back=back, c=c, tp=tp):
                    out_copy(ci, (c - back) % 2, row0 * 0, tp * MOE_PIECE).wait()
        zbuf[...] = jnp.zeros(zbuf.shape, zbuf.dtype)
        first_tail = tail_ref[0]
        n_total = out_hbm.shape[0] // MOE_PIECE

        def zcopy(p):
            return pltpu.make_async_copy(zbuf, out_hbm.at[pl.ds(pl.multiple_of(p * MOE_PIECE, MOE_PIECE), MOE_PIECE)],
                                         zsem.at[0])

        def zstart(p, c):
            zcopy(p).start()
            return c

        def zwait(p, c):
            zcopy(p).wait()
            return c
        lax.fori_loop(first_tail, n_total, zstart, 0)
        lax.fori_loop(first_tail, n_total, zwait, 0)


def _up_tile(x, w_view, b_refs):
    tf = b_refs[0].shape[-1]
    gu = jnp.dot(x, w_view[...].astype(BF16), preferred_element_type=F32)
    g = jnp.minimum(gu[:, :tf] + b_refs[0][0], SWIGLU_LIMIT)
    up = jnp.clip(gu[:, tf:] + b_refs[1][0], -SWIGLU_LIMIT, SWIGLU_LIMIT)
    return (up + 1.0) * (g * _sigmoid(SWIGLU_ALPHA * g))


def _down_tile(x, w_view, b_refs):
    return jnp.dot(x, w_view[...].astype(BF16), preferred_element_type=F32) + b_refs[0][0]


def _grouped_rows(tables, tile_pieces, xs, weights, col_blocks, biases, n_out, tn, out_dtype, tile_fn, name):
    wrow, wnp, tail = tables
    n_rows, k = xs.shape
    n_w = len(weights)
    n_tiles = len(tile_pieces)
    grid = (weights[0].shape[0], n_out // tn)
    b_specs = [pl.BlockSpec((1, 1, tn), lambda wi, j, *_, cb=cb: (wi, 0, cb + j)) for cb in col_blocks]
    scratch = [pltpu.VMEM((MOE_CAP_PIECES * MOE_PIECE, k), BF16)]
    scratch += [pltpu.VMEM((2, tp * MOE_PIECE, tn), out_dtype) for tp in tile_pieces]
    scratch += [pltpu.VMEM((MOE_PIECE, n_out), out_dtype),
                pltpu.VMEM((W_RING, k, n_w * tn), F32)]
    scratch += [pltpu.SemaphoreType.DMA((1,)),
                pltpu.SemaphoreType.DMA((n_tiles, 2)),
                pltpu.SemaphoreType.DMA((1,)),
                pltpu.SemaphoreType.DMA((n_w, W_RING)),
                pltpu.SMEM((n_tiles,), I32)]
    return pl.pallas_call(
        functools.partial(_grouped_rows_kernel, n_w=n_w, col_blocks=tuple(col_blocks), grid=grid, tn=tn,
                          tile_pieces=tuple(tile_pieces), tile_fn=tile_fn),
        grid_spec=pltpu.PrefetchScalarGridSpec(
            num_scalar_prefetch=3,
            grid=grid,
            in_specs=[pl.BlockSpec(memory_space=pl.ANY)] * (1 + n_w) + b_specs,
            out_specs=pl.BlockSpec(memory_space=pl.ANY),
            scratch_shapes=scratch),
        out_shape=jax.ShapeDtypeStruct((n_rows, n_out), out_dtype),
        compiler_params=_cparams(("arbitrary", "arbitrary")),
        name=name,
    )(wrow, wnp, tail, xs, *weights, *biases)


def _moe_up(tables, tile_pieces, xs, w_gate_up, b_gate_up, tf=256):
    n_exp, d, two_f = w_gate_up.shape
    f = two_f // 2
    b3 = b_gate_up.reshape(n_exp, 1, two_f)
    return _grouped_rows(tables, tile_pieces, xs, [w_gate_up, w_gate_up], [0, f // tf], [b3, b3], f, tf, BF16,
                         _up_tile, "moe_up")


def _moe_down(tables, tile_pieces, act, w_down, b_down, tn=512):
    n_exp, f, d = w_down.shape
    return _grouped_rows(tables, tile_pieces, act, [w_down], [0], [b_down.reshape(n_exp, 1, d)], d, tn, F32,
                         _down_tile, "moe_down")


def _combine_kernel(dest_ref, o_hbm, gate_ref, hp_ref, hs_ref, g_ref, yp_ref, ys_ref, buf, sem, *, tm,
                    n_prompt_tiles):
    i = pl.program_id(0)
    n = pl.num_programs(0)

    def issue(tile, slot):
        def one(r, carry):
            for kk in range(TOP_K):
                p = dest_ref[(tile * tm + r) * TOP_K + kk]
                pltpu.make_async_copy(o_hbm.at[pl.ds(p, 1), :], buf.at[slot, kk, pl.ds(r, 1), :],
                                      sem.at[slot]).start()
            return carry
        lax.fori_loop(0, tm, one, 0, unroll=2)

    @pl.when(i == 0)
    def _():
        issue(0, 0)

    @pl.when(i + 1 < n)
    def _():
        issue(i + 1, (i + 1) % 2)

    slot = i % 2
    for kk in range(TOP_K):
        pltpu.make_async_copy(o_hbm.at[pl.ds(0, tm), :], buf.at[slot, kk], sem.at[slot]).wait()

    def finish(h_ref, y_ref):
        rc = min(SUBLANES, tm)

        def rows_pass(r, carry):
            rows = pl.ds(pl.multiple_of(r * rc, rc), rc)
            gates = gate_ref[rows, :]
            acc = h_ref[rows, :]
            for kk in range(TOP_K):
                acc = acc + gates[:, kk:kk + 1] * buf[slot, kk, rows, :]
            y_ref[rows, :] = _rms(acc, g_ref[...])
            return carry
        lax.fori_loop(0, tm // rc, rows_pass, 0, unroll=2)

    @pl.when(i < n_prompt_tiles)
    def _():
        finish(hp_ref, yp_ref)

    @pl.when(i >= n_prompt_tiles)
    def _():
        finish(hs_ref, ys_ref)


def _combine(dest_flat, o_sorted, gates, hp, hs, g_final):
    tp, d = hp.shape
    ts = hs.shape[0]
    tm = min(128, ts)
    assert tp % tm == 0 and ts % tm == 0
    npt, nst = tp // tm, ts // tm
    p_blk = lambda: pl.BlockSpec((tm, d), lambda i, dref: (jnp.minimum(i, npt - 1), 0))
    s_blk = lambda: pl.BlockSpec((tm, d), lambda i, dref: (jnp.maximum(i - npt, 0), 0))
    return pl.pallas_call(
        functools.partial(_combine_kernel, tm=tm, n_prompt_tiles=npt),
        grid_spec=pltpu.PrefetchScalarGridSpec(
            num_scalar_prefetch=1,
            grid=(npt + nst,),
            in_specs=[pl.BlockSpec(memory_space=pl.ANY),
                      pl.BlockSpec((tm, LANES), lambda i, dref: (i, 0)),
                      p_blk(), s_blk(),
                      pl.BlockSpec((1, d), lambda i, dref: (0, 0))],
            out_specs=[p_blk(), s_blk()],
            scratch_shapes=[pltpu.VMEM((2, TOP_K, tm, d), F32), pltpu.SemaphoreType.DMA((2,))]),
        out_shape=[jax.ShapeDtypeStruct((tp, d), F32), jax.ShapeDtypeStruct((ts, d), F32)],
        compiler_params=_cparams(("arbitrary",)),
        name="moe_combine",
    )(dest_flat, o_sorted, gates, hp, hs, g_final.reshape(1, d))


def _mixer_front(x2, g_mix, w_in_t, b_i, b_f, n_heads, n_main):
    d = x2.shape[1]
    wgt = jnp.pad(w_in_t[n_main:], ((0, LANES - 2 * n_heads), (0, 0)))
    b_gate = jnp.pad(jnp.concatenate([b_i, b_f]), (0, LANES - 2 * n_heads)).reshape(1, LANES)
    xn, gates = _norm_gates(x2, g_mix.reshape(1, d), wgt, b_gate)
    u = _proj([xn], w_in_t, n_main, w_is_nk=True)
    return u, gates


def _moe(hp, hs, g_ffn, w_router_t, b_router, w_gate_up, b_gate_up, w_down, b_down, g_final):
    tp, d = hp.shape
    ts = hs.shape[0]
    t = tp + ts
    n_exp = w_router_t.shape[0]
    wrt = jnp.pad(w_router_t, ((0, LANES - n_exp), (0, 0)))
    br = jnp.pad(b_router, (0, LANES - n_exp), constant_values=NEG_BIG).reshape(1, LANES)
    tok3, idx_o, gate_o, rank_o, cnt_o = _router(hp, hs, g_ffn.reshape(1, d), wrt, br, tm=min(256, ts))

    n_rows = _round_up(t * TOP_K + n_exp * (MOE_PIECE - 1), GATHER_TILE)
    counts = cnt_o[0, :n_exp]
    pieces = ((counts + MOE_PIECE - 1) // MOE_PIECE).astype(I32)
    first_piece = jnp.cumsum(pieces) - pieces
    tables = ((first_piece * MOE_PIECE).astype(I32), pieces, jnp.sum(pieces).reshape(1).astype(I32))

    dest = (first_piece[idx_o[:, :TOP_K]] * MOE_PIECE + rank_o[:, :TOP_K]).reshape(-1)
    token_of_pair = jnp.repeat(jnp.arange(t, dtype=I32), TOP_K)
    src = jnp.zeros((n_rows,), I32).at[dest].set(token_of_pair)

    even_pieces = min(-(-(t * TOP_K) // (n_exp * MOE_PIECE)), MOE_CAP_PIECES)
    tile_pieces = tuple(sorted({even_pieces, 2, 1}, reverse=True))

    xs = _gather_rows(src, tok3, n_rows, GATHER_TILE)
    act = _moe_up(tables, tile_pieces, xs, w_gate_up, b_gate_up)
    o_sorted = _moe_down(tables, tile_pieces, act, w_down, b_down)
    return _combine(dest, o_sorted, gate_o, hp, hs, g_final)


def kernel(x_prompt, x_sample, state_conv, state_C, state_n, state_m, g_mix, w_in, b_i, b_f, w_dw, b_dw,
           conv_ln_g, conv_ln_b, mlstm_norm_g, w_out, g_ffn, w_router, b_router, w_gate_up, b_gate_up,
           w_down, b_down, g_final):
    b, s, d = x_prompt.shape
    bd, sd, _ = x_sample.shape
    depth = w_in.shape[0]
    assert depth == 1, "the final norm is fused into the MoE combine of the single layer"
    n_heads, dk, dv = state_C.shape[2], state_C.shape[3], state_C.shape[4]
    c_conv = state_conv.shape[3]
    n_main = 2 * c_conv + 2 * n_heads * dk + 2 * n_heads * dv
    l = 0
    w_in_t = jnp.swapaxes(w_in[l], 0, 1)
    w_router_t = jnp.swapaxes(w_router[l], 0, 1)

    xp2 = x_prompt.reshape(b * s, d)
    up, gates_p = _mixer_front(xp2, g_mix[l], w_in_t, b_i[l], b_f[l], n_heads, n_main)
    up3 = up.reshape(b, s, n_main)
    conv_p, cbp = _conv_seq(up3, w_dw[l], b_dw[l], conv_ln_g[l], conv_ln_b[l], c_conv)
    hm_p, c_p, n_p, m_p = _mlstm(up3, gates_p.reshape(b, s, LANES), mlstm_norm_g[l], None, n_heads, dk, dv, c_conv)
    hp = _proj([conv_p.reshape(b * s, c_conv), hm_p.reshape(b * s, n_heads * dv)], w_out[l], d, res=xp2)

    xs2 = x_sample.reshape(bd * sd, d)
    us, gates_s = _mixer_front(xs2, g_mix[l], w_in_t, b_i[l], b_f[l], n_heads, n_main)
    us3 = us.reshape(bd, sd, n_main)
    ug = jnp.transpose(us3[:, :, :2 * c_conv], (1, 0, 2))
    state_t = jnp.transpose(state_conv[l], (1, 0, 2))
    conv_s_t, cbs_t = _conv_step(ug, state_t, w_dw[l], b_dw[l], conv_ln_g[l], conv_ln_b[l], c_conv)
    conv_s = jnp.transpose(conv_s_t, (1, 0, 2)).reshape(bd * sd, c_conv)
    cbs = jnp.transpose(cbs_t, (1, 0, 2))
    hm_s, c_s, n_s, m_s = _mlstm(us3, gates_s.reshape(bd, sd, LANES), mlstm_norm_g[l],
                                 (state_C[l], state_n[l], state_m[l]), n_heads, dk, dv, c_conv)
    hs = _proj([conv_s, hm_s.reshape(bd * sd, n_heads * dv)], w_out[l], d, res=xs2)

    yp, ys = _moe(hp, hs, g_ffn[l], w_router_t, b_router[l], w_gate_up[l], b_gate_up[l], w_down[l], b_down[l],
                  g_final)
    return (yp.reshape(b, s, d), ys.reshape(bd, sd, d), cbp[None], c_p[None], n_p[None], m_p[None],
            cbs[None], c_s[None], n_s[None], m_s[None])
```

```python
import functools

import jax
import jax.numpy as jnp
from jax import lax
from jax.experimental import pallas as pl
from jax.experimental.pallas import tpu as pltpu

F32 = jnp.float32
BF16 = jnp.bfloat16
I32 = jnp.int32

NORM_EPS = 1e-5
TOP_K = 4
SWIGLU_LIMIT = 7.0
SWIGLU_ALPHA = 1.702
LANES = 128
SUBLANES = 8
VMEM_LIMIT_BYTES = 56 * 1024 * 1024
NEG_BIG = -1e30

MOE_PIECE = 128
MOE_CAP_PIECES = 16
W_RING = 3
GATHER_TILE = 256


def _round_up(a, b):
    return (a + b - 1) // b * b


def _cparams(sem):
    return pltpu.CompilerParams(dimension_semantics=sem, vmem_limit_bytes=VMEM_LIMIT_BYTES)


def _sigmoid(x):
    return 1.0 / (1.0 + jnp.exp(-x))


def _log_sigmoid(x):
    return jnp.minimum(x, 0.0) - jnp.log(1.0 + jnp.exp(-jnp.abs(x)))


def _rms(x, g):
    return x * lax.rsqrt(jnp.mean(x * x, axis=-1, keepdims=True) + NORM_EPS) * g


_NT = (((1,), (1,)), ((), ()))


def _norm_gates_kernel(x_ref, g_ref, wgt_ref, bg_ref, xn_ref, gt_ref):
    yb = _rms(x_ref[...], g_ref[...]).astype(BF16)
    xn_ref[...] = yb
    gt_ref[...] = lax.dot_general(yb, wgt_ref[...].astype(BF16), _NT, preferred_element_type=F32) + bg_ref[...]


def _norm_gates(x, g, wgt, bg):
    tn, d = x.shape
    tm = min(512, tn)
    return pl.pallas_call(
        _norm_gates_kernel,
        grid=(tn // tm,),
        in_specs=[pl.BlockSpec((tm, d), lambda i: (i, 0)),
                  pl.BlockSpec((1, d), lambda i: (0, 0)),
                  pl.BlockSpec((LANES, d), lambda i: (0, 0)),
                  pl.BlockSpec((1, LANES), lambda i: (0, 0))],
        out_specs=[pl.BlockSpec((tm, d), lambda i: (i, 0)),
                   pl.BlockSpec((tm, LANES), lambda i: (i, 0))],
        out_shape=[jax.ShapeDtypeStruct((tn, d), BF16), jax.ShapeDtypeStruct((tn, LANES), F32)],
        compiler_params=_cparams(("arbitrary",)),
        name="norm_gates",
    )(x, g, wgt, bg)


def _proj_kernel(*refs, n_lhs, has_res, w_is_nk):
    lhs_refs = refs[:n_lhs]
    w_ref = refs[n_lhs]
    res_ref = refs[n_lhs + 1] if has_res else None
    o_ref = refs[-1]
    acc = None
    k0 = 0
    for r in lhs_refs:
        kk = r.shape[1]
        lhs = r[...].astype(BF16)
        if w_is_nk:
            part = lax.dot_general(lhs, w_ref[:, k0:k0 + kk].astype(BF16), _NT, preferred_element_type=F32)
        else:
            part = jnp.dot(lhs, w_ref[k0:k0 + kk, :].astype(BF16), preferred_element_type=F32)
        acc = part if acc is None else acc + part
        k0 += kk
    if has_res:
        acc = acc + res_ref[...]
    o_ref[...] = acc


def _proj(lhs_list, w, n_out, res=None, w_is_nk=False, tm=1024, tn=512, rows_outer=False):
    rows = lhs_list[0].shape[0]
    k_total = w.shape[1] if w_is_nk else w.shape[0]
    assert sum(a.shape[1] for a in lhs_list) == k_total and n_out % tn == 0
    tm = min(tm, rows)
    assert rows % tm == 0
    if rows_outer:
        grid = (rows // tm, n_out // tn)
        ij = lambda a, b: (a, b)
    else:
        grid = (n_out // tn, rows // tm)
        ij = lambda a, b: (b, a)
    in_specs = [pl.BlockSpec((tm, a.shape[1]), lambda a_, b_: (ij(a_, b_)[0], 0)) for a in lhs_list]
    if w_is_nk:
        in_specs.append(pl.BlockSpec((tn, k_total), lambda a_, b_: (ij(a_, b_)[1], 0)))
    else:
        in_specs.append(pl.BlockSpec((k_total, tn), lambda a_, b_: (0, ij(a_, b_)[1])))
    args = list(lhs_list) + [w]
    if res is not None:
        in_specs.append(pl.BlockSpec((tm, tn), lambda a_, b_: ij(a_, b_)))
        args.append(res)
    return pl.pallas_call(
        functools.partial(_proj_kernel, n_lhs=len(lhs_list), has_res=res is not None, w_is_nk=w_is_nk),
        grid=grid,
        in_specs=in_specs,
        out_specs=pl.BlockSpec((tm, tn), lambda a_, b_: ij(a_, b_)),
        out_shape=jax.ShapeDtypeStruct((rows, n_out), F32),
        compiler_params=_cparams(("arbitrary", "arbitrary")),
        name="proj",
    )(*args)


def _layernorm_swish(y, g, b):
    mu = jnp.mean(y, axis=-1, keepdims=True)
    yc = y - mu
    var = jnp.mean(yc * yc, axis=-1, keepdims=True)
    z = yc * lax.rsqrt(var + NORM_EPS) * g + b
    return z * _sigmoid(z)


def _conv_seq_kernel(ua_ref, ub_ref, wdw_ref, bdw_ref, lng_ref, lnb_ref, out_ref, nbuf_ref, full_ref, y_ref,
                     *, tl, kw, cc):
    halo = kw - 1
    padh = _round_up(halo, SUBLANES)
    off = padh - halo
    c = ua_ref.shape[-1]
    l = pl.program_id(1)

    @pl.when(l == 0)
    def _():
        full_ref[off:padh, :] = jnp.zeros((halo, c), F32)
        full_ref[padh + tl:, :] = jnp.zeros((SUBLANES, c), F32)

    full_ref[padh:padh + tl, :] = ua_ref[0] * _sigmoid(ub_ref[0])

    by_residue = {}
    for w in range(kw):
        by_residue.setdefault((off + w) % SUBLANES, []).append(w)
    for c0 in range(0, c, cc):
        acc = jnp.zeros((tl, cc), F32) + bdw_ref[:, c0:c0 + cc]
        for r, taps in sorted(by_residue.items()):
            z = None
            for w in taps:
                base = off + w - r
                term = full_ref[base:base + tl + SUBLANES, c0:c0 + cc] * wdw_ref[w:w + 1, c0:c0 + cc]
                z = term if z is None else z + term
            acc = acc + z[r:r + tl]
        y_ref[:, c0:c0 + cc] = acc

    out_ref[0] = _layernorm_swish(y_ref[...], lng_ref[...], lnb_ref[...]).astype(BF16)

    tail = full_ref[tl + off:tl + padh, :]

    @pl.when(l == pl.num_programs(1) - 1)
    def _():
        nbuf_ref[0] = tail

    full_ref[off:padh, :] = tail


def _conv_seq(u3, w_dw, b_dw, ln_g, ln_b, c):
    b, l, _ = u3.shape
    kw = w_dw.shape[0]
    halo = kw - 1
    tl = 64
    assert l % tl == 0
    padh = _round_up(halo, SUBLANES)
    return pl.pallas_call(
        functools.partial(_conv_seq_kernel, tl=tl, kw=kw, cc=min(256, c)),
        grid=(b, l // tl),
        in_specs=[pl.BlockSpec((1, tl, c), lambda bi, li: (bi, li, 0)),
                  pl.BlockSpec((1, tl, c), lambda bi, li: (bi, li, 1)),
                  pl.BlockSpec((kw, c), lambda bi, li: (0, 0))] + [pl.BlockSpec((1, c), lambda bi, li: (0, 0))] * 3,
        out_specs=[pl.BlockSpec((1, tl, c), lambda bi, li: (bi, li, 0)),
                   pl.BlockSpec((1, halo, c), lambda bi, li: (bi, 0, 0))],
        out_shape=[jax.ShapeDtypeStruct((b, l, c), BF16), jax.ShapeDtypeStruct((b, halo, c), F32)],
        scratch_shapes=[pltpu.VMEM((padh + tl + SUBLANES, c), F32), pltpu.VMEM((tl, c), F32)],
        compiler_params=_cparams(("arbitrary", "arbitrary")),
        name="conv_seq",
    )(u3, u3, w_dw, b_dw.reshape(1, c), ln_g.reshape(1, c), ln_b.reshape(1, c))


def _conv_step_kernel(ua_ref, ub_ref, st_ref, wdw_ref, bdw_ref, lng_ref, lnb_ref, out_ref, nst_ref, y_ref,
                      *, ls, kw, cc):
    halo = kw - 1
    j = pl.program_id(0)
    glu = [ua_ref[t] * _sigmoid(ub_ref[t]) for t in range(ls)]

    def window(idx):
        return st_ref[idx] if idx < halo else glu[idx - halo]

    for t in range(ls):
        acc = jnp.zeros(glu[0].shape, F32) + bdw_ref[...]
        for w in range(kw):
            acc = acc + window(t + w) * wdw_ref[w:w + 1, :]
        y_ref[j, t] = acc
    for idx in range(halo):
        nst_ref[idx] = window(ls + idx)

    @pl.when(j == pl.num_programs(0) - 1)
    def _():
        for t in range(ls):
            y = jnp.concatenate([y_ref[jj, t] for jj in range(y_ref.shape[0])], axis=-1)
            out_ref[t] = _layernorm_swish(y, lng_ref[...], lnb_ref[...]).astype(BF16)


def _conv_step(ug, state_t, w_dw, b_dw, ln_g, ln_b, c):
    ls, b, _ = ug.shape
    kw = w_dw.shape[0]
    halo = kw - 1
    cc = min(256, c)
    nj = c // cc
    return pl.pallas_call(
        functools.partial(_conv_step_kernel, ls=ls, kw=kw, cc=cc),
        grid=(nj,),
        in_specs=[pl.BlockSpec((ls, b, cc), lambda j: (0, 0, j)),
                  pl.BlockSpec((ls, b, cc), lambda j: (0, 0, nj + j)),
                  pl.BlockSpec((halo, b, cc), lambda j: (0, 0, j)),
                  pl.BlockSpec((kw, cc), lambda j: (0, j)),
                  pl.BlockSpec((1, cc), lambda j: (0, j)),
                  pl.BlockSpec((1, c), lambda j: (0, 0)),
                  pl.BlockSpec((1, c), lambda j: (0, 0))],
        out_specs=[pl.BlockSpec((ls, b, c), lambda j: (0, 0, 0)),
                   pl.BlockSpec((halo, b, cc), lambda j: (0, 0, j))],
        out_shape=[jax.ShapeDtypeStruct((ls, b, c), BF16), jax.ShapeDtypeStruct((halo, b, c), F32)],
        scratch_shapes=[pltpu.VMEM((nj, ls, b, cc), F32)],
        compiler_params=_cparams(("arbitrary",)),
        name="conv_step",
    )(ug, ug, state_t, w_dw, b_dw.reshape(1, c), ln_g.reshape(1, c), ln_b.reshape(1, c))


def _mlstm_kernel(*refs, lc, dk, dv, n_heads, has_state):
    if has_state:
        (q_ref, k_ref, v_ref, o_ref, gc_ref, gr_ref, ng_ref, c0_ref, n0_ref, m0_ref,
         hm_ref, co_ref, no_ref, mo_ref, c_s, n_s, m_s) = refs
    else:
        (q_ref, k_ref, v_ref, o_ref, gc_ref, gr_ref, ng_ref,
         hm_ref, co_ref, no_ref, mo_ref, c_s, n_s, m_s) = refs
    j = pl.program_id(1)

    @pl.when(j == 0)
    def _():
        if has_state:
            c_s[...] = c0_ref[0]
            n_s[...] = n0_ref[0]
            m_s[...] = m0_ref[0]
        else:
            c_s[...] = jnp.zeros(c_s.shape, F32)
            n_s[...] = jnp.zeros(n_s.shape, F32)
            m_s[...] = jnp.zeros(m_s.shape, F32)

    gc = gc_ref[0]
    gr = gr_ref[0]
    ri = lax.broadcasted_iota(I32, (lc, lc), 0)
    ci = lax.broadcasted_iota(I32, (lc, lc), 1)
    causal = ri >= ci

    for h in range(n_heads):
        q = q_ref[0, :, h * dk:(h + 1) * dk] * (dk ** -0.5)
        kf = k_ref[0, :, h * dk:(h + 1) * dk]
        qb = q.astype(BF16)
        kb = kf.astype(BF16)
        vb = v_ref[0, :, h * dv:(h + 1) * dv].astype(BF16)

        gi_col = gc[:, h:h + 1]
        gf_col = gc[:, n_heads + h:n_heads + h + 1]
        gi_row = gr[h:h + 1, :]
        gf_row = gr[n_heads + h:n_heads + h + 1, :]
        lf_col = _log_sigmoid(gf_col)
        lf_row = _log_sigmoid(gf_row)
        a_col = jnp.sum(jnp.where(causal, lf_row, 0.0), axis=1, keepdims=True)
        a_row = jnp.sum(jnp.where(ri <= ci, lf_col, 0.0), axis=0, keepdims=True)

        m_prev = m_s[h]
        d_log = jnp.where(causal, a_col - a_row + gi_row, -jnp.inf)
        inter = a_col + m_prev
        m_row = jnp.maximum(inter, jnp.max(d_log, axis=1, keepdims=True))
        w_intra = jnp.exp(d_log - m_row)
        w_inter = jnp.exp(inter - m_row)

        s = lax.dot_general(qb, kb, _NT, preferred_element_type=F32) * w_intra
        c_prev = c_s[h]
        n_prev = n_s[h]
        q_c = jnp.dot(qb, c_prev.astype(BF16), preferred_element_type=F32)
        num = jnp.dot(s.astype(BF16), vb, preferred_element_type=F32) + w_inter * q_c
        q_n = jnp.sum(q * n_prev, axis=1, keepdims=True)
        den_raw = jnp.sum(s, axis=1, keepdims=True) + w_inter * q_n
        den = jnp.maximum(jnp.abs(den_raw), jnp.exp(-m_row))
        hh = num / den

        hn = _rms(hh, ng_ref[:, h * dv:(h + 1) * dv])
        hm_ref[0, :, h * dv:(h + 1) * dv] = (hn * _sigmoid(o_ref[0, :, h * dv:(h + 1) * dv])).astype(BF16)

        a_end = a_col[lc - 1:lc, :]
        m_new = m_row[lc - 1:lc, :]
        decay = jnp.exp(a_end + m_prev - m_new)
        w_end = jnp.exp(a_end - a_col + gi_col - m_new)
        kw = kf * w_end
        c_new = decay * c_prev + lax.dot_general(kw.astype(BF16), vb, (((0,), (0,)), ((), ())),
                                                 preferred_element_type=F32)
        n_new = decay * n_prev + jnp.sum(kw, axis=0, keepdims=True)
        c_s[h] = c_new
        n_s[h] = n_new
        m_s[h] = m_new

    @pl.when(j == pl.num_programs(1) - 1)
    def _():
        co_ref[0] = c_s[...]
        no_ref[0] = n_s[...]
        mo_ref[0] = m_s[...]


def _mlstm(u3, gates3, norm_g, state, n_heads, dk, dv, c_conv):
    b, l, _ = u3.shape
    lc = 256 if l % 256 == 0 else l
    nc = l // lc
    wq, wv = n_heads * dk, n_heads * dv
    assert (2 * c_conv) % wq == 0 and (2 * c_conv + 2 * wq) % wv == 0
    q0 = 2 * c_conv // wq
    v0 = (2 * c_conv + 2 * wq) // wv
    n_gate_rows = _round_up(2 * n_heads, SUBLANES)
    gates_row = jnp.transpose(gates3[:, :, :n_gate_rows], (0, 2, 1))
    has_state = state is not None
    in_specs = [pl.BlockSpec((1, lc, wq), lambda bi, ji: (bi, ji, q0)),
                pl.BlockSpec((1, lc, wq), lambda bi, ji: (bi, ji, q0 + 1)),
                pl.BlockSpec((1, lc, wv), lambda bi, ji: (bi, ji, v0)),
                pl.BlockSpec((1, lc, wv), lambda bi, ji: (bi, ji, v0 + 1)),
                pl.BlockSpec((1, lc, LANES), lambda bi, ji: (bi, ji, 0)),
                pl.BlockSpec((1, n_gate_rows, lc), lambda bi, ji: (bi, 0, ji)),
                pl.BlockSpec((1, wv), lambda bi, ji: (0, 0))]
    args = [u3, u3, u3, u3, gates3, gates_row, norm_g.reshape(1, wv)]
    state_specs = lambda: [pl.BlockSpec((1, n_heads, dk, dv), lambda bi, ji: (bi, 0, 0, 0)),
                           pl.BlockSpec((1, n_heads, 1, dk), lambda bi, ji: (bi, 0, 0, 0)),
                           pl.BlockSpec((1, n_heads, 1, 1), lambda bi, ji: (bi, 0, 0, 0))]
    if has_state:
        c0, n0, m0 = state
        in_specs += state_specs()
        args += [c0, n0.reshape(b, n_heads, 1, dk), m0.reshape(b, n_heads, 1, 1)]
    hm, c_f, n_f, m_f = pl.pallas_call(
        functools.partial(_mlstm_kernel, lc=lc, dk=dk, dv=dv, n_heads=n_heads, has_state=has_state),
        grid=(b, nc),
        in_specs=in_specs,
        out_specs=[pl.BlockSpec((1, lc, wv), lambda bi, ji: (bi, ji, 0))] + state_specs(),
        out_shape=[jax.ShapeDtypeStruct((b, l, wv), BF16),
                   jax.ShapeDtypeStruct((b, n_heads, dk, dv), F32),
                   jax.ShapeDtypeStruct((b, n_heads, 1, dk), F32),
                   jax.ShapeDtypeStruct((b, n_heads, 1, 1), F32)],
        scratch_shapes=[pltpu.VMEM((n_heads, dk, dv), F32), pltpu.VMEM((n_heads, 1, dk), F32),
                        pltpu.VMEM((n_heads, 1, 1), F32)],
        compiler_params=_cparams(("arbitrary", "arbitrary")),
        name="mlstm",
    )(*args)
    return hm, c_f, n_f.reshape(b, n_heads, dk), m_f.reshape(b, n_heads)


def _router_kernel(hp_ref, hs_ref, g_ref, wrt_ref, br_ref, tok_ref, idx_ref, gate_ref, rank_ref, cnt_ref,
                   carry_ref, *, n_prompt_tiles):
    i = pl.program_id(0)

    @pl.when(i == 0)
    def _():
        carry_ref[...] = jnp.zeros(carry_ref.shape, F32)

    x = jnp.where(i < n_prompt_tiles, hp_ref[...], hs_ref[...])
    tok = _rms(x, g_ref[...])
    tok_ref[...] = tok.reshape(tok_ref.shape)
    w_f = wrt_ref[...]
    t_hi = tok.astype(BF16)
    t_lo = (tok - t_hi.astype(F32)).astype(BF16)
    w_hi = w_f.astype(BF16)
    w_lo = (w_f - w_hi.astype(F32)).astype(BF16)
    lg = (lax.dot_general(t_hi, w_hi, _NT, preferred_element_type=F32)
          + (lax.dot_general(t_hi, w_lo, _NT, preferred_element_type=F32)
             + lax.dot_general(t_lo, w_hi, _NT, preferred_element_type=F32))) + br_ref[...]
    tm = lg.shape[0]
    lane = lax.broadcasted_iota(I32, lg.shape, 1).astype(F32)
    vals, idxs = [], []
    for _ in range(TOP_K):
        mx = jnp.max(lg, axis=1, keepdims=True)
        ik = jnp.min(jnp.where(lg == mx, lane, float(LANES)), axis=1, keepdims=True)
        vals.append(mx)
        idxs.append(ik)
        lg = jnp.where(lane == ik, -jnp.inf, lg)
    exps = [jnp.exp(v - vals[0]) for v in vals]
    ssum = exps[0]
    for e in exps[1:]:
        ssum = ssum + e
    gates = [e / ssum for e in exps]

    multi = jnp.zeros(lg.shape, F32)
    for ik in idxs:
        multi = multi + (lane == ik).astype(F32)
    ri = lax.broadcasted_iota(I32, (tm, tm), 0)
    ci = lax.broadcasted_iota(I32, (tm, tm), 1)
    below = (ri > ci).astype(BF16)
    cum = jnp.dot(below, multi.astype(BF16), preferred_element_type=F32) + carry_ref[...]
    ranks = [jnp.sum(jnp.where(lane == ik, cum, 0.0), axis=1, keepdims=True) for ik in idxs]
    carry = carry_ref[...] + jnp.sum(multi, axis=0, keepdims=True)
    carry_ref[...] = carry
    cnt_ref[...] = carry.astype(I32)

    idx_o = jnp.zeros(lg.shape, F32)
    gate_o = jnp.zeros(lg.shape, F32)
    rank_o = jnp.zeros(lg.shape, F32)
    for kk in range(TOP_K):
        sel = lane == float(kk)
        idx_o = jnp.where(sel, idxs[kk], idx_o)
        gate_o = jnp.where(sel, gates[kk], gate_o)
        rank_o = jnp.where(sel, ranks[kk], rank_o)
    idx_ref[...] = idx_o.astype(I32)
    gate_ref[...] = gate_o
    rank_ref[...] = rank_o.astype(I32)


def _router(hp, hs, g, wrt, br, tm):
    tp, d = hp.shape
    ts = hs.shape[0]
    assert tp % tm == 0 and ts % tm == 0
    npt, nst = tp // tm, ts // tm
    t = tp + ts
    dg = d // LANES
    blk = lambda w: pl.BlockSpec((tm, w), lambda i: (i, 0))
    return pl.pallas_call(
        functools.partial(_router_kernel, n_prompt_tiles=npt),
        grid=(npt + nst,),
        in_specs=[pl.BlockSpec((tm, d), lambda i: (jnp.minimum(i, npt - 1), 0)),
                  pl.BlockSpec((tm, d), lambda i: (jnp.maximum(i - npt, 0), 0)),
                  pl.BlockSpec((1, d), lambda i: (0, 0)),
                  pl.BlockSpec((LANES, d), lambda i: (0, 0)),
                  pl.BlockSpec((1, LANES), lambda i: (0, 0))],
        out_specs=[pl.BlockSpec((tm, dg, LANES), lambda i: (i, 0, 0)), blk(LANES), blk(LANES), blk(LANES),
                   pl.BlockSpec((1, LANES), lambda i: (0, 0))],
        out_shape=[jax.ShapeDtypeStruct((t, dg, LANES), F32), jax.ShapeDtypeStruct((t, LANES), I32),
                   jax.ShapeDtypeStruct((t, LANES), F32), jax.ShapeDtypeStruct((t, LANES), I32),
                   jax.ShapeDtypeStruct((1, LANES), I32)],
        scratch_shapes=[pltpu.VMEM((1, LANES), F32)],
        compiler_params=_cparams(("arbitrary",)),
        name="router",
    )(hp, hs, g, wrt, br)


def _gather_rows_kernel(src_ref, tok_hbm, out_ref, buf, sem, *, tm, depth):
    i = pl.program_id(0)
    n = pl.num_programs(0)

    def tile_copy(slot):
        return pltpu.make_async_copy(tok_hbm.at[pl.ds(0, tm)], buf.at[slot], sem.at[slot])

    def issue(tile):
        slot = tile % depth

        def one(r, carry):
            t = src_ref[tile * tm + r]
            pltpu.make_async_copy(tok_hbm.at[t], buf.at[slot, r], sem.at[slot]).start()
            return carry
        lax.fori_loop(0, tm, one, 0, unroll=8)

    @pl.when(i == 0)
    def _():
        for ahead in range(depth - 1):
            @pl.when(ahead < n)
            def _(ahead=ahead):
                issue(ahead)

    @pl.when(i + depth - 1 < n)
    def _():
        issue(i + depth - 1)

    slot = i % depth
    tile_copy(slot).wait()
    out_ref[...] = buf[slot].reshape(out_ref.shape).astype(BF16)


def _gather_rows(src, tok3, n_rows, tm, depth=3):
    _, dg, lanes = tok3.shape
    return pl.pallas_call(
        functools.partial(_gather_rows_kernel, tm=tm, depth=depth),
        grid_spec=pltpu.PrefetchScalarGridSpec(
            num_scalar_prefetch=1,
            grid=(n_rows // tm,),
            in_specs=[pl.BlockSpec(memory_space=pl.ANY)],
            out_specs=pl.BlockSpec((tm, dg * lanes), lambda i, src_ref: (i, 0)),
            scratch_shapes=[pltpu.VMEM((depth, tm, dg, lanes), F32), pltpu.SemaphoreType.DMA((depth,))]),
        out_shape=jax.ShapeDtypeStruct((n_rows, dg * lanes), BF16),
        compiler_params=_cparams(("arbitrary",)),
        name="moe_gather",
    )(src, tok3)


def _grouped_rows_kernel(wrow_ref, wnp_ref, tail_ref, x_hbm, *refs, n_w, col_blocks, grid, tn, tile_pieces, tile_fn):
    n_tiles = len(tile_pieces)
    w_hbm = refs[:n_w]
    b_refs = refs[n_w:2 * n_w]
    out_hbm = refs[2 * n_w]
    scratch = refs[2 * n_w + 1:]
    xbuf = scratch[0]
    obufs = scratch[1:1 + n_tiles]
    zbuf = scratch[1 + n_tiles]
    wbuf = scratch[2 + n_tiles]
    xsem, osem, zsem, wsem, cnt_ref = scratch[3 + n_tiles:]
    n_work, nj = grid
    n_steps = n_work * nj
    wi = pl.program_id(0)
    j = pl.program_id(1)
    step = wi * nj + j
    n_pieces = wnp_ref[wi]
    row0 = wrow_ref[wi]
    is_first = step == 0
    is_last = step == n_steps - 1

    @pl.when(is_first)
    def _():
        for ci in range(n_tiles):
            cnt_ref[ci] = 0

    def weight_copy(a, s_wi, s_j, slot):
        col = pl.multiple_of((col_blocks[a] + s_j) * tn, tn)
        return pltpu.make_async_copy(w_hbm[a].at[s_wi, :, pl.ds(col, tn)],
                                     wbuf.at[slot, :, pl.ds(a * tn, tn)], wsem.at[a, slot])

    def fetch(s):
        s_wi = s // nj
        s_j = s % nj

        @pl.when(wnp_ref[s_wi] > 0)
        def _():
            for a in range(n_w):
                weight_copy(a, s_wi, s_j, s % W_RING).start()

    @pl.when(is_first)
    def _():
        for ahead in range(min(W_RING - 1, n_steps)):
            fetch(ahead)

    @pl.when(step + (W_RING - 1) < n_steps)
    def _():
        fetch(step + (W_RING - 1))

    w_slot = step % W_RING
    w_view = wbuf.at[w_slot]

    @pl.when(n_pieces > 0)
    def _():
        for a in range(n_w):
            weight_copy(a, wi, j, w_slot).wait()

    col0 = pl.multiple_of(j * tn, tn)
    cap_rows = xbuf.shape[0]
    cap_pieces = cap_rows // MOE_PIECE
    n_chunks = (n_pieces + cap_pieces - 1) // cap_pieces

    def out_copy(ci, slot, row, rows):
        dst = out_hbm.at[pl.ds(pl.multiple_of(row, MOE_PIECE), rows), pl.ds(col0, tn)]
        return pltpu.make_async_copy(obufs[ci].at[slot], dst, osem.at[ci, slot])

    def emit(ci, chunk_row0, r0):
        rows = tile_pieces[ci] * MOE_PIECE
        r0 = pl.multiple_of(r0, MOE_PIECE)
        val = tile_fn(xbuf[pl.ds(r0, rows), :], w_view, b_refs)
        c = cnt_ref[ci]
        slot = c % 2

        @pl.when(c >= 2)
        def _():
            out_copy(ci, slot, chunk_row0 + r0, rows).wait()

        obufs[ci][slot] = val.astype(obufs[ci].dtype)
        out_copy(ci, slot, chunk_row0 + r0, rows).start()
        cnt_ref[ci] = c + 1

    def chunk(ck, carry):
        chunk_row0 = row0 + ck * cap_rows
        chunk_pieces = jnp.minimum(n_pieces - ck * cap_pieces, cap_pieces)

        def piece_copy(p):
            off = pl.multiple_of(p * MOE_PIECE, MOE_PIECE)
            return pltpu.make_async_copy(x_hbm.at[pl.ds(pl.multiple_of(chunk_row0 + off, MOE_PIECE), MOE_PIECE)],
                                         xbuf.at[pl.ds(off, MOE_PIECE)], xsem.at[0])

        @pl.when(jnp.logical_or(j == 0, n_chunks > 1))
        def _():
            def start(p, c):
                piece_copy(p).start()
                return c

            def wait(p, c):
                piece_copy(p).wait()
                return c
            lax.fori_loop(0, chunk_pieces, start, 0)
            lax.fori_loop(0, chunk_pieces, wait, 0)

        remaining = chunk_pieces
        base = 0
        for ci, tp in enumerate(tile_pieces):
            n_t = remaining // tp

            def body(t, c, ci=ci, base=base, tp=tp):
                emit(ci, chunk_row0, base + t * (tp * MOE_PIECE))
                return c
            lax.fori_loop(0, n_t, body, 0)
            base = base + n_t * (tp * MOE_PIECE)
            remaining = remaining - n_t * tp
        return carry

    lax.fori_loop(0, n_chunks, chunk, 0)

    @pl.when(is_last)
    def _():
        for ci, tp in enumerate(tile_pieces):
            c = cnt_ref[ci]
            for back in (1, 2):
                @pl.when(c >= back)
                def _(ci=ci, back=back, c=c, tp=tp):
                    out_copy(ci, (c - back) % 2, row0 * 0, tp * MOE_PIECE).wait()
        zbuf[...] = jnp.zeros(zbuf.shape, zbuf.dtype)
        first_tail = tail_ref[0]
        n_total = out_hbm.shape[0] // MOE_PIECE

        def zcopy(p):
            return pltpu.make_async_copy(zbuf, out_hbm.at[pl.ds(pl.multiple_of(p * MOE_PIECE, MOE_PIECE), MOE_PIECE)],
                                         zsem.at[0])

        def zstart(p, c):
            zcopy(p).start()
            return c

        def zwait(p, c):
            zcopy(p).wait()
            return c
        lax.fori_loop(first_tail, n_total, zstart, 0)
        lax.fori_loop(first_tail, n_total, zwait, 0)


def _up_tile(x, w_view, b_refs):
    tf = b_refs[0].shape[-1]
    gu = jnp.dot(x, w_view[...].astype(BF16), preferred_element_type=F32)
    g = jnp.minimum(gu[:, :tf] + b_refs[0][0], SWIGLU_LIMIT)
    up = jnp.clip(gu[:, tf:] + b_refs[1][0], -SWIGLU_LIMIT, SWIGLU_LIMIT)
    return (up + 1.0) * (g * _sigmoid(SWIGLU_ALPHA * g))


def _down_tile(x, w_view, b_refs):
    return jnp.dot(x, w_view[...].astype(BF16), preferred_element_type=F32) + b_refs[0][0]


def _grouped_rows(tables, tile_pieces, xs, weights, col_blocks, biases, n_out, tn, out_dtype, tile_fn, name):
    wrow, wnp, tail = tables
    n_rows, k = xs.shape
    n_w = len(weights)
    n_tiles = len(tile_pieces)
    grid = (weights[0].shape[0], n_out // tn)
    b_specs = [pl.BlockSpec((1, 1, tn), lambda wi, j, *_, cb=cb: (wi, 0, cb + j)) for cb in col_blocks]
    scratch = [pltpu.VMEM((MOE_CAP_PIECES * MOE_PIECE, k), BF16)]
    scratch += [pltpu.VMEM((2, tp * MOE_PIECE, tn), out_dtype) for tp in tile_pieces]
    scratch += [pltpu.VMEM((MOE_PIECE, n_out), out_dtype),
                pltpu.VMEM((W_RING, k, n_w * tn), F32)]
    scratch += [pltpu.SemaphoreType.DMA((1,)),
                pltpu.SemaphoreType.DMA((n_tiles, 2)),
                pltpu.SemaphoreType.DMA((1,)),
                pltpu.SemaphoreType.DMA((n_w, W_RING)),
                pltpu.SMEM((n_tiles,), I32)]
    return pl.pallas_call(
        functools.partial(_grouped_rows_kernel, n_w=n_w, col_blocks=tuple(col_blocks), grid=grid, tn=tn,
                          tile_pieces=tuple(tile_pieces), tile_fn=tile_fn),
        grid_spec=pltpu.PrefetchScalarGridSpec(
            num_scalar_prefetch=3,
            grid=grid,
            in_specs=[pl.BlockSpec(memory_space=pl.ANY)] * (1 + n_w) + b_specs,
            out_specs=pl.BlockSpec(memory_space=pl.ANY),
            scratch_shapes=scratch),
        out_shape=jax.ShapeDtypeStruct((n_rows, n_out), out_dtype),
        compiler_params=_cparams(("arbitrary", "arbitrary")),
        name=name,
    )(wrow, wnp, tail, xs, *weights, *biases)


def _moe_up(tables, tile_pieces, xs, w_gate_up, b_gate_up, tf=256):
    n_exp, d, two_f = w_gate_up.shape
    f = two_f // 2
    b3 = b_gate_up.reshape(n_exp, 1, two_f)
    return _grouped_rows(tables, tile_pieces, xs, [w_gate_up, w_gate_up], [0, f // tf], [b3, b3], f, tf, BF16,
                         _up_tile, "moe_up")


def _moe_down(tables, tile_pieces, act, w_down, b_down, tn=512):
    n_exp, f, d = w_down.shape
    return _grouped_rows(tables, tile_pieces, act, [w_down], [0], [b_down.reshape(n_exp, 1, d)], d, tn, F32,
                         _down_tile, "moe_down")


def _combine_kernel(dest_ref, o_hbm, gate_ref, hp_ref, hs_ref, g_ref, yp_ref, ys_ref, buf, sem, *, tm,
                    n_prompt_tiles):
    i = pl.program_id(0)
    n = pl.num_programs(0)

    def issue(tile, slot):
        def one(r, carry):
            for kk in range(TOP_K):
                p = dest_ref[(tile * tm + r) * TOP_K + kk]
                pltpu.make_async_copy(o_hbm.at[pl.ds(p, 1), :], buf.at[slot, kk, pl.ds(r, 1), :],
                                      sem.at[slot]).start()
            return carry
        lax.fori_loop(0, tm, one, 0, unroll=2)

    @pl.when(i == 0)
    def _():
        issue(0, 0)

    @pl.when(i + 1 < n)
    def _():
        issue(i + 1, (i + 1) % 2)

    slot = i % 2
    for kk in range(TOP_K):
        pltpu.make_async_copy(o_hbm.at[pl.ds(0, tm), :], buf.at[slot, kk], sem.at[slot]).wait()

    gates = gate_ref[...]

    def finish(h_ref, y_ref):
        acc = h_ref[...]
        for kk in range(TOP_K):
            acc = acc + gates[:, kk:kk + 1] * buf[slot, kk]
        y_ref[...] = _rms(acc, g_ref[...])

    @pl.when(i < n_prompt_tiles)
    def _():
        finish(hp_ref, yp_ref)

    @pl.when(i >= n_prompt_tiles)
    def _():
        finish(hs_ref, ys_ref)


def _combine(dest_flat, o_sorted, gates, hp, hs, g_final):
    tp, d = hp.shape
    ts = hs.shape[0]
    tm = min(128, ts)
    assert tp % tm == 0 and ts % tm == 0
    npt, nst = tp // tm, ts // tm
    p_blk = lambda: pl.BlockSpec((tm, d), lambda i, dref: (jnp.minimum(i, npt - 1), 0))
    s_blk = lambda: pl.BlockSpec((tm, d), lambda i, dref: (jnp.maximum(i - npt, 0), 0))
    return pl.pallas_call(
        functools.partial(_combine_kernel, tm=tm, n_prompt_tiles=npt),
        grid_spec=pltpu.PrefetchScalarGridSpec(
            num_scalar_prefetch=1,
            grid=(npt + nst,),
            in_specs=[pl.BlockSpec(memory_space=pl.ANY),
                      pl.BlockSpec((tm, LANES), lambda i, dref: (i, 0)),
                      p_blk(), s_blk(),
                      pl.BlockSpec((1, d), lambda i, dref: (0, 0))],
            out_specs=[p_blk(), s_blk()],
            scratch_shapes=[pltpu.VMEM((2, TOP_K, tm, d), F32), pltpu.SemaphoreType.DMA((2,))]),
        out_shape=[jax.ShapeDtypeStruct((tp, d), F32), jax.ShapeDtypeStruct((ts, d), F32)],
        compiler_params=_cparams(("arbitrary",)),
        name="moe_combine",
    )(dest_flat, o_sorted, gates, hp, hs, g_final.reshape(1, d))


def _mixer_front(x2, g_mix, w_in_t, b_i, b_f, n_heads, n_main):
    d = x2.shape[1]
    wgt = jnp.pad(w_in_t[n_main:], ((0, LANES - 2 * n_heads), (0, 0)))
    b_gate = jnp.pad(jnp.concatenate([b_i, b_f]), (0, LANES - 2 * n_heads)).reshape(1, LANES)
    xn, gates = _norm_gates(x2, g_mix.reshape(1, d), wgt, b_gate)
    u = _proj([xn], w_in_t, n_main, w_is_nk=True, tm=2048, tn=256, rows_outer=True)
    return u, gates


def _moe(hp, hs, g_ffn, w_router_t, b_router, w_gate_up, b_gate_up, w_down, b_down, g_final):
    tp, d = hp.shape
    ts = hs.shape[0]
    t = tp + ts
    n_exp = w_router_t.shape[0]
    wrt = jnp.pad(w_router_t, ((0, LANES - n_exp), (0, 0)))
    br = jnp.pad(b_router, (0, LANES - n_exp), constant_values=NEG_BIG).reshape(1, LANES)
    tok3, idx_o, gate_o, rank_o, cnt_o = _router(hp, hs, g_ffn.reshape(1, d), wrt, br, tm=min(256, ts))

    n_rows = _round_up(t * TOP_K + n_exp * (MOE_PIECE - 1), GATHER_TILE)
    counts = cnt_o[0, :n_exp]
    pieces = ((counts + MOE_PIECE - 1) // MOE_PIECE).astype(I32)
    first_piece = jnp.cumsum(pieces) - pieces
    tables = ((first_piece * MOE_PIECE).astype(I32), pieces, jnp.sum(pieces).reshape(1).astype(I32))

    dest = (first_piece[idx_o[:, :TOP_K]] * MOE_PIECE + rank_o[:, :TOP_K]).reshape(-1)
    token_of_pair = jnp.repeat(jnp.arange(t, dtype=I32), TOP_K)
    src = jnp.zeros((n_rows,), I32).at[dest].set(token_of_pair)

    even_pieces = min(-(-(t * TOP_K) // (n_exp * MOE_PIECE)), MOE_CAP_PIECES)
    tile_pieces = tuple(sorted({even_pieces, 2, 1}, reverse=True))

    xs = _gather_rows(src, tok3, n_rows, GATHER_TILE)
    act = _moe_up(tables, tile_pieces, xs, w_gate_up, b_gate_up)
    o_sorted = _moe_down(tables, tile_pieces, act, w_down, b_down)
    return _combine(dest, o_sorted, gate_o, hp, hs, g_final)


def kernel(x_prompt, x_sample, state_conv, state_C, state_n, state_m, g_mix, w_in, b_i, b_f, w_dw, b_dw,
           conv_ln_g, conv_ln_b, mlstm_norm_g, w_out, g_ffn, w_router, b_router, w_gate_up, b_gate_up,
           w_down, b_down, g_final):
    b, s, d = x_prompt.shape
    bd, sd, _ = x_sample.shape
    depth = w_in.shape[0]
    assert depth == 1, "the final norm is fused into the MoE combine of the single layer"
    n_heads, dk, dv = state_C.shape[2], state_C.shape[3], state_C.shape[4]
    c_conv = state_conv.shape[3]
    n_main = 2 * c_conv + 2 * n_heads * dk + 2 * n_heads * dv
    l = 0
    w_in_t = jnp.swapaxes(w_in[l], 0, 1)
    w_router_t = jnp.swapaxes(w_router[l], 0, 1)

    xp2 = x_prompt.reshape(b * s, d)
    up, gates_p = _mixer_front(xp2, g_mix[l], w_in_t, b_i[l], b_f[l], n_heads, n_main)
    up3 = up.reshape(b, s, n_main)
    conv_p, cbp = _conv_seq(up3, w_dw[l], b_dw[l], conv_ln_g[l], conv_ln_b[l], c_conv)
    hm_p, c_p, n_p, m_p = _mlstm(up3, gates_p.reshape(b, s, LANES), mlstm_norm_g[l], None, n_heads, dk, dv, c_conv)
    hp = _proj([conv_p.reshape(b * s, c_conv), hm_p.reshape(b * s, n_heads * dv)], w_out[l], d, res=xp2)

    xs2 = x_sample.reshape(bd * sd, d)
    us, gates_s = _mixer_front(xs2, g_mix[l], w_in_t, b_i[l], b_f[l], n_heads, n_main)
    us3 = us.reshape(bd, sd, n_main)
    ug = jnp.transpose(us3[:, :, :2 * c_conv], (1, 0, 2))
    state_t = jnp.transpose(state_conv[l], (1, 0, 2))
    conv_s_t, cbs_t = _conv_step(ug, state_t, w_dw[l], b_dw[l], conv_ln_g[l], conv_ln_b[l], c_conv)
    conv_s = jnp.transpose(conv_s_t, (1, 0, 2)).reshape(bd * sd, c_conv)
    cbs = jnp.transpose(cbs_t, (1, 0, 2))
    hm_s, c_s, n_s, m_s = _mlstm(us3, gates_s.reshape(bd, sd, LANES), mlstm_norm_g[l],
                                 (state_C[l], state_n[l], state_m[l]), n_heads, dk, dv, c_conv)
    hs = _proj([conv_s, hm_s.reshape(bd * sd, n_heads * dv)], w_out[l], d, res=xs2)

    yp, ys = _moe(hp, hs, g_ffn[l], w_router_t, b_router[l], w_gate_up[l], b_gate_up[l], w_down[l], b_down[l],
                  g_final)
    return (yp.reshape(b, s, d), ys.reshape(bd, sd, d), cbp[None], c_p[None], n_p[None], m_p[None],
            cbs[None], c_s[None], n_s[None], m_s[None])
```
